```python
import jax
import jax.numpy as jnp
from jax import lax
import numpy as np


D_MODEL = 1024
BATCH = 4
SEQ = 4096
DEPTH = 1

N_HEADS = 8
N_KV_HEADS = 2
HEAD_DIM = 64
ATTN_WIDTH = N_HEADS * HEAD_DIM
KV_WIDTH = N_KV_HEADS * HEAD_DIM
WINDOW = 128
ATTN_BLOCK = 128
CONV_CH = 512
CONV_WIDTH = 31
Q_END = ATTN_WIDTH
K_END = Q_END + KV_WIDTH
V_END = K_END + KV_WIDTH
CONV_END = V_END + 2 * CONV_CH
IN_WIDTH = CONV_END + 2 * D_MODEL
N_EXPERTS = 32
TOP_K = 4
D_EXPERT = D_MODEL
SWIGLU_ALPHA = 1.702
SWIGLU_LIMIT = 7.0
MOE_BLOCK = 128
LN_EPS = 1e-5
DEEPNORM_ALPHA = (2 * DEPTH) ** 0.25
DEEPNORM_BETA = (8 * DEPTH) ** -0.25

kernel_name = "hybrid_swa_sink_conformer_moe_deepnorm"


def layer_norm(x, g, b):
    xf = x.astype(jnp.float32)
    mu = jnp.mean(xf, axis=-1, keepdims=True)
    var = jnp.mean(jnp.square(xf - mu), axis=-1, keepdims=True)
    y = (xf - mu) * lax.rsqrt(var + LN_EPS) * g.astype(jnp.float32) + b.astype(jnp.float32)
    return y.astype(x.dtype)


def band_mask(n_blocks):
    qi = np.arange(ATTN_BLOCK)[:, None]
    kj = np.arange(2 * ATTN_BLOCK)[None, :]
    delta = qi + ATTN_BLOCK - kj
    band = (delta >= 0) & (delta < WINDOW)
    blk = np.arange(n_blocks)[:, None, None]
    return band[None] & ((blk > 0) | (kj >= ATTN_BLOCK)[None])


def sliding_window_sink_attention(q, k, v, sinks):
    B, S, _ = q.shape
    nb = S // ATTN_BLOCK
    G = N_HEADS // N_KV_HEADS
    qb = q.reshape(B, nb, ATTN_BLOCK, N_KV_HEADS, G, HEAD_DIM)
    kb = k.reshape(B, nb, ATTN_BLOCK, N_KV_HEADS, HEAD_DIM)
    vb = v.reshape(B, nb, ATTN_BLOCK, N_KV_HEADS, HEAD_DIM)
    pad = ((0, 0), (1, 0), (0, 0), (0, 0), (0, 0))
    kk = jnp.concatenate([jnp.pad(kb[:, :-1], pad), kb], axis=2)
    vv = jnp.concatenate([jnp.pad(vb[:, :-1], pad), vb], axis=2)
    s = jnp.einsum('bnqhgd,bnkhd->bnhgqk', qb, kk,
                   preferred_element_type=jnp.float32) * (HEAD_DIM ** -0.5)
    mask = band_mask(nb)
    s = jnp.where(mask[None, :, None, None], s, -1e30)
    sink = sinks.astype(jnp.float32).reshape(N_KV_HEADS, G)[None, None, :, :, None, None]
    sink = jnp.broadcast_to(sink, s.shape[:-1] + (1,))
    p = jax.nn.softmax(jnp.concatenate([s, sink], axis=-1), axis=-1)[..., :-1]
    o = jnp.einsum('bnhgqk,bnkhd->bnqhgd', p.astype(v.dtype), vv)
    return o.reshape(B, S, ATTN_WIDTH)


def conformer_conv(c_in, conv_w, conv_b, ln_g, ln_b, w_br, b_br):
    a = c_in[..., :CONV_CH] * jax.nn.sigmoid(c_in[..., CONV_CH:])
    y = lax.conv_general_dilated(
        a, conv_w[:, None, :].astype(a.dtype), window_strides=(1,),
        padding=[(CONV_WIDTH - 1, 0)], dimension_numbers=('NWC', 'WIO', 'NWC'),
        feature_group_count=CONV_CH) + conv_b
    y = jax.nn.silu(layer_norm(y, ln_g, ln_b))
    return y @ w_br + b_br


def clamped_swiglu(up):
    x_glu = jnp.minimum(up[..., ::2], SWIGLU_LIMIT)
    x_lin = jnp.clip(up[..., 1::2], -SWIGLU_LIMIT, SWIGLU_LIMIT)
    return x_glu * jax.nn.sigmoid(SWIGLU_ALPHA * x_glu) * (x_lin + 1.0)


def moe_ffn(h, w_router, b_router, w_up, b_up, w_down, b_down):
    T, D = h.shape
    logits = (h @ w_router + b_router).astype(jnp.float32)
    top_val, top_idx = lax.top_k(logits, TOP_K)
    gates = jax.nn.softmax(top_val, axis=-1)
    A = T * TOP_K
    e_flat = top_idx.reshape(A)
    tok_flat = jnp.arange(A, dtype=jnp.int32) // TOP_K
    g_flat = gates.reshape(A)
    order = jnp.argsort(e_flat, stable=True)
    e_sorted = e_flat[order]
    counts = jnp.zeros((N_EXPERTS,), jnp.int32).at[e_flat].add(1)
    padded = (counts + MOE_BLOCK - 1) // MOE_BLOCK * MOE_BLOCK
    pad_end = jnp.cumsum(padded)
    pad_start = pad_end - padded
    start = jnp.cumsum(counts) - counts
    rank = jnp.arange(A, dtype=jnp.int32) - start[e_sorted]
    dest = pad_start[e_sorted] + rank
    P = A + N_EXPERTS * MOE_BLOCK
    n_blk = P // MOE_BLOCK
    slot_tok = jnp.zeros((P,), jnp.int32).at[dest].set(tok_flat[order])
    slot_gate = jnp.zeros((P,), jnp.float32).at[dest].set(g_flat[order])
    blk_expert = jnp.minimum(
        jnp.searchsorted(pad_end, jnp.arange(n_blk, dtype=jnp.int32) * MOE_BLOCK, side='right'),
        N_EXPERTS - 1)
    xs = h[slot_tok].reshape(n_blk, MOE_BLOCK, D)

    def expert_block(args):
        xb, e = args
        act = clamped_swiglu(xb @ w_up[e] + b_up[e])
        return act @ w_down[e] + b_down[e]

    ys = lax.map(expert_block, (xs, blk_expert)).reshape(P, D)
    return jnp.zeros_like(h).at[slot_tok].add(ys * slot_gate[:, None].astype(ys.dtype))


def setup_inputs(seed: int = 0) -> dict:
    key = jax.random.key(seed)
    ks = jax.random.split(key, 24)

    def nrm(k, shape, scale):
        return jax.random.normal(k, shape, jnp.float32) * scale

    x = nrm(ks[0], (BATCH, SEQ, D_MODEL), 1.0)
    col_scale = jnp.ones((IN_WIDTH,), jnp.float32).at[K_END:V_END].set(DEEPNORM_BETA)
    w_in = nrm(ks[1], (DEPTH, D_MODEL, IN_WIDTH), D_MODEL ** -0.5) * col_scale
    b_in = nrm(ks[2], (DEPTH, IN_WIDTH), 0.02)
    attn_sinks = nrm(ks[3], (DEPTH, N_HEADS), 0.5)
    w_attn_br = nrm(ks[4], (DEPTH, ATTN_WIDTH, D_MODEL), ATTN_WIDTH ** -0.5 * DEEPNORM_BETA)
    conv_w = nrm(ks[5], (DEPTH, CONV_WIDTH, CONV_CH), CONV_WIDTH ** -0.5)
    conv_b = nrm(ks[6], (DEPTH, CONV_CH), 0.02)
    conv_ln_g = 1.0 + nrm(ks[7], (DEPTH, CONV_CH), 0.02)
    conv_ln_b = nrm(ks[8], (DEPTH, CONV_CH), 0.02)
    w_conv_br = nrm(ks[9], (DEPTH, CONV_CH, D_MODEL), CONV_CH ** -0.5 * DEEPNORM_BETA)
    b_conv_br = nrm(ks[10], (DEPTH, D_MODEL), 0.02)
    w_o = nrm(ks[11], (DEPTH, D_MODEL, D_MODEL), D_MODEL ** -0.5 * DEEPNORM_BETA)
    ln1_g = 1.0 + nrm(ks[12], (DEPTH, D_MODEL), 0.02)
    ln1_b = nrm(ks[13], (DEPTH, D_MODEL), 0.02)
    w_router = nrm(ks[14], (DEPTH, D_MODEL, N_EXPERTS), D_MODEL ** -0.5)
    b_router = nrm(ks[15], (DEPTH, N_EXPERTS), 0.01)
    w_up = nrm(ks[16], (DEPTH, N_EXPERTS, D_MODEL, 2 * D_EXPERT), D_MODEL ** -0.5)
    b_up = nrm(ks[17], (DEPTH, N_EXPERTS, 2 * D_EXPERT), 0.02)
    w_down = nrm(ks[18], (DEPTH, N_EXPERTS, D_EXPERT, D_MODEL), D_EXPERT ** -0.5 * DEEPNORM_BETA)
    b_down = nrm(ks[19], (DEPTH, N_EXPERTS, D_MODEL), 0.02)
    ln2_g = 1.0 + nrm(ks[20], (DEPTH, D_MODEL), 0.02)
    ln2_b = nrm(ks[21], (DEPTH, D_MODEL), 0.02)
    return {"x": x, "w_in": w_in, "b_in": b_in, "attn_sinks": attn_sinks, "w_attn_br": w_attn_br,
            "conv_w": conv_w, "conv_b": conv_b, "conv_ln_g": conv_ln_g, "conv_ln_b": conv_ln_b,
            "w_conv_br": w_conv_br, "b_conv_br": b_conv_br, "w_o": w_o, "ln1_g": ln1_g, "ln1_b": ln1_b,
            "w_router": w_router, "b_router": b_router, "w_up": w_up, "b_up": b_up,
            "w_down": w_down, "b_down": b_down, "ln2_g": ln2_g, "ln2_b": ln2_b}


def reference(x, w_in, b_in, attn_sinks, w_attn_br, conv_w, conv_b, conv_ln_g, conv_ln_b,
              w_conv_br, b_conv_br, w_o, ln1_g, ln1_b, w_router, b_router, w_up, b_up,
              w_down, b_down, ln2_g, ln2_b):
    B, S, D = x.shape
    for l in range(DEPTH):
        u = x @ w_in[l] + b_in[l]
        q = u[..., :Q_END]
        k = u[..., Q_END:K_END]
        v = u[..., K_END:V_END]
        c_in = u[..., V_END:CONV_END]
        gate = jax.nn.sigmoid(u[..., CONV_END:])
        y_attn = sliding_window_sink_attention(q, k, v, attn_sinks[l]) @ w_attn_br[l]
        y_conv = conformer_conv(c_in, conv_w[l], conv_b[l], conv_ln_g[l], conv_ln_b[l],
                                w_conv_br[l], b_conv_br[l])
        merged = gate[..., :D] * y_attn + gate[..., D:] * y_conv
        x = layer_norm(DEEPNORM_ALPHA * x + merged @ w_o[l], ln1_g[l], ln1_b[l])
        h = x.reshape(B * S, D)
        y_moe = moe_ffn(h, w_router[l], b_router[l], w_up[l], b_up[l], w_down[l], b_down[l])
        x = layer_norm(DEEPNORM_ALPHA * x + y_moe.reshape(B, S, D), ln2_g[l], ln2_b[l])
    return x
```

```python
import functools

import jax
import jax.numpy as jnp
from jax import lax
from jax.experimental import pallas as pl
from jax.experimental.pallas import tpu as pltpu

F32 = jnp.float32
BF16 = jnp.bfloat16

D_MODEL = 1024
N_HEADS = 8
N_KV_HEADS = 2
HEAD_DIM = 64
ATTN_WIDTH = N_HEADS * HEAD_DIM
KV_WIDTH = N_KV_HEADS * HEAD_DIM
WINDOW = 128
ATTN_BLOCK = 128
CONV_CH = 512
CONV_WIDTH = 31
Q_END = ATTN_WIDTH
K_END = Q_END + KV_WIDTH
V_END = K_END + KV_WIDTH
CONV_END = V_END + 2 * CONV_CH
IN_WIDTH = CONV_END + 2 * D_MODEL
N_EXPERTS = 32
TOP_K = 4
D_EXPERT = D_MODEL
SWIGLU_ALPHA = 1.702
SWIGLU_LIMIT = 7.0
LN_EPS = 1e-5
DEPTH = 1
DEEPNORM_ALPHA = (2 * DEPTH) ** 0.25

LANES = 128
NEG_BIG = -1e30
TOKEN_TILE = 512
CONV_HALO = 32
EXPERT_TILE = 256
COMBINE_TILE = 256
VMEM_LIMIT = 56 * 1024 * 1024


def _layer_norm(z, g, b):
    mu = jnp.mean(z, axis=-1, keepdims=True)
    zc = z - mu
    var = jnp.mean(zc * zc, axis=-1, keepdims=True)
    return zc * lax.rsqrt(var + LN_EPS) * g + b


def _cparams(sem):
    return pltpu.CompilerParams(dimension_semantics=sem, vmem_limit_bytes=VMEM_LIMIT)


def _inproj_kernel(x_ref, w_ref, b_ref, q_ref, k_ref, v_ref, a_ref, g_ref):
    x = x_ref[...].astype(BF16)

    def proj(lo, hi):
        return jnp.dot(x, w_ref[:, lo:hi], preferred_element_type=F32) + b_ref[:, lo:hi]

    qkv = proj(0, V_END)
    q_ref[...] = qkv[:, :Q_END].astype(BF16)
    k_ref[...] = qkv[:, Q_END:K_END].astype(BF16)
    v_ref[...] = qkv[:, K_END:V_END].astype(BF16)
    c = proj(V_END, CONV_END)
    a_ref[...] = c[:, :CONV_CH] * jax.nn.sigmoid(c[:, CONV_CH:])
    g_ref[...] = jax.nn.sigmoid(proj(CONV_END, IN_WIDTH))


def _inproj(x2, w_in, b_in):
    T = x2.shape[0]
    row = lambda w: pl.BlockSpec((TOKEN_TILE, w), lambda i: (i, 0))
    full = lambda s: pl.BlockSpec(s, lambda i: (0, 0))
    return pl.pallas_call(
        _inproj_kernel,
        grid=(T // TOKEN_TILE,),
        in_specs=[row(D_MODEL), full((D_MODEL, IN_WIDTH)), full((1, IN_WIDTH))],
        out_specs=[row(ATTN_WIDTH), row(KV_WIDTH), row(KV_WIDTH), row(CONV_CH), row(2 * D_MODEL)],
        out_shape=[jax.ShapeDtypeStruct((T, ATTN_WIDTH), BF16),
                   jax.ShapeDtypeStruct((T, KV_WIDTH), BF16),
                   jax.ShapeDtypeStruct((T, KV_WIDTH), BF16),
                   jax.ShapeDtypeStruct((T, CONV_CH), F32),
                   jax.ShapeDtypeStruct((T, 2 * D_MODEL), F32)],
        compiler_params=_cparams(("parallel",)),
        name="inproj",
    )(x2, w_in, b_in)


def _attn_kernel(sink_ref, q_ref, kc_ref, kp_ref, vc_ref, vp_ref, o_ref):
    i = pl.program_id(1)
    q = q_ref[...]
    kk = jnp.concatenate([kp_ref[...], kc_ref[...]], axis=0)
    vv = jnp.concatenate([vp_ref[...], vc_ref[...]], axis=0)
    qi = lax.broadcasted_iota(jnp.int32, (ATTN_BLOCK, 2 * ATTN_BLOCK), 0)
    kj = lax.broadcasted_iota(jnp.int32, (ATTN_BLOCK, 2 * ATTN_BLOCK), 1)
    delta = qi + ATTN_BLOCK - kj
    mask = (delta >= 0) & (delta < WINDOW) & ((kj >= ATTN_BLOCK) | (i > 0))
    group = N_HEADS // N_KV_HEADS
    outs = []
    for h in range(N_HEADS):
        g = h // group
        qh = q[:, h * HEAD_DIM:(h + 1) * HEAD_DIM]
        kh = kk[:, g * HEAD_DIM:(g + 1) * HEAD_DIM]
        vh = vv[:, g * HEAD_DIM:(g + 1) * HEAD_DIM]
        s = lax.dot_general(qh, kh, (((1,), (1,)), ((), ())),
                            preferred_element_type=F32) * (HEAD_DIM ** -0.5)
        s = jnp.where(mask, s, NEG_BIG)
        sink = sink_ref[h]
        m = jnp.maximum(jnp.max(s, axis=-1, keepdims=True), sink)
        p = jnp.exp(s - m)
        denom = jnp.sum(p, axis=-1, keepdims=True) + jnp.exp(sink - m)
        p = p / denom
        outs.append(jnp.dot(p.astype(BF16), vh, preferred_element_type=F32))
    o_ref[...] = jnp.concatenate(outs, axis=1).astype(BF16)


def _attention(q, k, v, sinks, batch, seq):
    T = q.shape[0]
    nb = seq // ATTN_BLOCK
    cur = lambda w: pl.BlockSpec((ATTN_BLOCK, w), lambda b, i: (b * nb + i, 0))
    prev = lambda w: pl.BlockSpec((ATTN_BLOCK, w), lambda b, i: (b * nb + jnp.maximum(i - 1, 0), 0))
    return pl.pallas_call(
        _attn_kernel,
        grid=(batch, nb),
        in_specs=[pl.BlockSpec(memory_space=pltpu.SMEM),
                  cur(ATTN_WIDTH), cur(KV_WIDTH), prev(KV_WIDTH), cur(KV_WIDTH), prev(KV_WIDTH)],
        out_specs=cur(ATTN_WIDTH),
        out_shape=jax.ShapeDtypeStruct((T, ATTN_WIDTH), BF16),
        compiler_params=_cparams(("parallel", "parallel")),
        name="swa_attention",
    )(sinks, q, k, k, v, v)


def _conv_kernel(ap_ref, ac_ref, w_ref, cb_ref, g_ref, b_ref, o_ref, ext_ref):
    i = pl.program_id(1)
    ext_ref[0:CONV_HALO, :] = jnp.where(i > 0, ap_ref[...], 0.0)
    ext_ref[CONV_HALO:, :] = ac_ref[...]
    acc = jnp.zeros((TOKEN_TILE, CONV_CH), F32) + cb_ref[...]
    base = CONV_HALO - (CONV_WIDTH - 1)
    for j in range(CONV_WIDTH):
        acc = acc + ext_ref[base + j:base + j + TOKEN_TILE, :] * w_ref[j:j + 1, :]
    y = _layer_norm(acc, g_ref[...], b_ref[...])
    o_ref[...] = (y * jax.nn.sigmoid(y)).astype(BF16)


def _conv_branch(a, conv_w, conv_b, ln_g, ln_b, batch, seq):
    T = a.shape[0]
    nt = seq // TOKEN_TILE
    halo_per_tile = TOKEN_TILE // CONV_HALO
    cur = pl.BlockSpec((TOKEN_TILE, CONV_CH), lambda b, i: (b * nt + i, 0))
    prev = pl.BlockSpec((CONV_HALO, CONV_CH),
                        lambda b, i: (jnp.maximum((b * nt + i) * halo_per_tile - 1, 0), 0))
    full = lambda s: pl.BlockSpec(s, lambda b, i: (0, 0))
    return pl.pallas_call(
        _conv_kernel,
        grid=(batch, nt),
        in_specs=[prev, cur, full((CONV_WIDTH, CONV_CH)), full((1, CONV_CH)),
                  full((1, CONV_CH)), full((1, CONV_CH))],
        out_specs=cur,
        out_shape=jax.ShapeDtypeStruct((T, CONV_CH), BF16),
        scratch_shapes=[pltpu.VMEM((CONV_HALO + TOKEN_TILE, CONV_CH), F32)],
        compiler_params=_cparams(("parallel", "parallel")),
        name="conformer_conv",
    )(a, a, conv_w, conv_b, ln_g, ln_b)


def _split_bf16(x):
    hi = x.astype(BF16)
    lo = (x - hi.astype(F32)).astype(BF16)
    return hi, lo


def _merge_kernel(o_ref, yc_ref, g_ref, x_ref, wa_ref, wc_ref, bc_ref, wo_ref, lng_ref, lnb_ref,
                  wrh_ref, wrl_ref, br_ref, x1_ref, lg_ref):
    ya = jnp.dot(o_ref[...], wa_ref[...], preferred_element_type=F32)
    yc = jnp.dot(yc_ref[...], wc_ref[...], preferred_element_type=F32) + bc_ref[...]
    merged = g_ref[:, :D_MODEL] * ya + g_ref[:, D_MODEL:] * yc
    z = DEEPNORM_ALPHA * x_ref[...] + jnp.dot(merged.astype(BF16), wo_ref[...],
                                              preferred_element_type=F32)
    x1 = _layer_norm(z, lng_ref[...], lnb_ref[...])
    x1_ref[...] = x1
    hi, lo = _split_bf16(x1)
    wh = wrh_ref[...]
    lg = (jnp.dot(hi, wh, preferred_element_type=F32)
          + (jnp.dot(hi, wrl_ref[...], preferred_element_type=F32)
             + jnp.dot(lo, wh, preferred_element_type=F32)))
    lg_ref[...] = lg + br_ref[...]


def _merge(o, yc, g, x2, wa, wc, bc, wo, lng, lnb, wrh, wrl, br):
    T = x2.shape[0]
    row = lambda w: pl.BlockSpec((TOKEN_TILE, w), lambda i: (i, 0))
    full = lambda s: pl.BlockSpec(s, lambda i: (0, 0))
    return pl.pallas_call(
        _merge_kernel,
        grid=(T // TOKEN_TILE,),
        in_specs=[row(ATTN_WIDTH), row(CONV_CH), row(2 * D_MODEL), row(D_MODEL),
                  full((ATTN_WIDTH, D_MODEL)), full((CONV_CH, D_MODEL)), full((1, D_MODEL)),
                  full((D_MODEL, D_MODEL)), full((1, D_MODEL)), full((1, D_MODEL)),
                  full((D_MODEL, LANES)), full((D_MODEL, LANES)), full((1, LANES))],
        out_specs=[row(D_MODEL), row(LANES)],
        out_shape=[jax.ShapeDtypeStruct((T, D_MODEL), F32), jax.ShapeDtypeStruct((T, LANES), F32)],
        compiler_params=_cparams(("parallel",)),
        name="merge_ln1_router",
    )(o, yc, g, x2, wa, wc, bc, wo, lng, lnb, wrh, wrl, br)


def _route_kernel(lg_ref, idx_ref, gate_ref, rank_ref, cnt_ref, carry_ref):
    @pl.when(pl.program_id(0) == 0)
    def _():
        carry_ref[...] = jnp.zeros_like(carry_ref)

    work = lg_ref[...]
    shape = work.shape
    lane_i = lax.broadcasted_iota(jnp.int32, shape, 1)
    lane = lane_i.astype(F32)
    vals, hots, firsts = [], [], []
    for _ in range(TOP_K):
        m = jnp.max(work, axis=-1, keepdims=True)
        first = jnp.min(jnp.where(work == m, lane, float(LANES)), axis=-1, keepdims=True)
        hot = lane == first
        vals.append(m)
        hots.append(hot)
        firsts.append(first)
        work = jnp.where(hot, -jnp.inf, work)
    exps = [jnp.exp(v - vals[0]) for v in vals]
    denom = exps[0] + exps[1] + exps[2] + exps[3]
    sel = hots[0] | hots[1] | hots[2] | hots[3]
    sel_f = jnp.where(sel, 1.0, 0.0)
    row = lax.broadcasted_iota(jnp.int32, (shape[0], shape[0]), 0)
    col = lax.broadcasted_iota(jnp.int32, (shape[0], shape[0]), 1)
    earlier = jnp.where(row > col, 1.0, 0.0).astype(BF16)
    prefix = jnp.dot(earlier, sel_f.astype(BF16), preferred_element_type=F32) + carry_ref[...]
    idx = jnp.zeros(shape, F32)
    gate = jnp.zeros(shape, F32)
    rank = jnp.zeros(shape, F32)
    for k in range(TOP_K):
        here = lane_i == k
        rank_k = jnp.sum(jnp.where(hots[k], prefix, 0.0), axis=-1, keepdims=True)
        idx = jnp.where(here, firsts[k], idx)
        gate = jnp.where(here, exps[k] / denom, gate)
        rank = jnp.where(here, rank_k, rank)
    idx_ref[...] = idx.astype(jnp.int32)
    gate_ref[...] = gate
    rank_ref[...] = rank.astype(jnp.int32)
    carry_ref[...] = carry_ref[...] + jnp.sum(sel_f, axis=0, keepdims=True)
    cnt_ref[...] = carry_ref[...].astype(jnp.int32)


def _route(logits):
    T = logits.shape[0]
    row = pl.BlockSpec((TOKEN_TILE, LANES), lambda i: (i, 0))
    return pl.pallas_call(
        _route_kernel,
        grid=(T // TOKEN_TILE,),
        in_specs=[row],
        out_specs=[row, row, row, pl.BlockSpec((1, LANES), lambda i: (0, 0))],
        out_shape=[jax.ShapeDtypeStruct((T, LANES), jnp.int32),
                   jax.ShapeDtypeStruct((T, LANES), F32),
                   jax.ShapeDtypeStruct((T, LANES), jnp.int32),
                   jax.ShapeDtypeStruct((1, LANES), jnp.int32)],
        scratch_shapes=[pltpu.VMEM((1, LANES), F32)],
        compiler_params=_cparams(("arbitrary",)),
        name="route_topk_rank",
    )(logits)


def _row_copy(src_hbm, dst_vmem, sem, src_row, dst_row):
    return pltpu.make_async_copy(src_hbm.at[pl.ds(src_row, 1), :], dst_vmem.at[pl.ds(dst_row, 1), :], sem)


def _expert_kernel(te_ref, nu_ref, tok_ref, x1_hbm, wup_ref, bup_ref, wdn_ref, bdn_ref, ys_ref,
                   xbuf, sem):
    del te_ref
    s = pl.program_id(0)

    @pl.when(s < nu_ref[0])
    def _():
        def start(r, c):
            _row_copy(x1_hbm, xbuf, sem, tok_ref[0, 0, r], r).start()
            return c

        def wait(r, c):
            _row_copy(x1_hbm, xbuf, sem, tok_ref[0, 0, r], r).wait()
            return c

        lax.fori_loop(0, EXPERT_TILE, start, 0)
        lax.fori_loop(0, EXPERT_TILE, wait, 0)
        x = xbuf[...].astype(BF16)
        up = jnp.dot(x, wup_ref[0], preferred_element_type=F32) + bup_ref[0]
        glu = jnp.minimum(up[:, :D_EXPERT], SWIGLU_LIMIT)
        lin = jnp.clip(up[:, D_EXPERT:], -SWIGLU_LIMIT, SWIGLU_LIMIT)
        act = glu * jax.nn.sigmoid(SWIGLU_ALPHA * glu) * (lin + 1.0)
        ys_ref[...] = jnp.dot(act.astype(BF16), wdn_ref[0], preferred_element_type=F32) + bdn_ref[0]

    @pl.when(s >= nu_ref[0])
    def _():
        ys_ref[...] = jnp.zeros_like(ys_ref)


def _experts(tile_expert, n_used, slot_tok3, x1, wup, bup, wdn, bdn):
    n_tiles = slot_tok3.shape[0]
    grid_spec = pltpu.PrefetchScalarGridSpec(
        num_scalar_prefetch=2,
        grid=(n_tiles,),
        in_specs=[pl.BlockSpec((1, 1, EXPERT_TILE), lambda s, te, nu: (s, 0, 0), memory_space=pltpu.SMEM),
                  pl.BlockSpec(memory_space=pl.ANY),
                  pl.BlockSpec((1, D_MODEL, 2 * D_EXPERT), lambda s, te, nu: (te[s], 0, 0)),
                  pl.BlockSpec((1, 1, 2 * D_EXPERT), lambda s, te, nu: (te[s], 0, 0)),
                  pl.BlockSpec((1, D_EXPERT, D_MODEL), lambda s, te, nu: (te[s], 0, 0)),
                  pl.BlockSpec((1, 1, D_MODEL), lambda s, te, nu: (te[s], 0, 0))],
        out_specs=pl.BlockSpec((EXPERT_TILE, D_MODEL), lambda s, te, nu: (s, 0)),
        scratch_shapes=[pltpu.VMEM((EXPERT_TILE, D_MODEL), F32), pltpu.SemaphoreType.DMA(())],
    )
    return pl.pallas_call(
        _expert_kernel,
        grid_spec=grid_spec,
        out_shape=jax.ShapeDtypeStruct((n_tiles * EXPERT_TILE, D_MODEL), F32),
        compiler_params=_cparams(("arbitrary",)),
        name="expert_ffn",
    )(tile_expert, n_used, slot_tok3, x1, wup, bup, wdn, bdn)


def _combine_kernel(pos_ref, gate_ref, x1_ref, ys_hbm, lng_ref, lnb_ref, out_ref, buf, sem):
    def copy(r, k):
        return pltpu.make_async_copy(ys_hbm.at[pl.ds(pos_ref[0, 0, r * TOP_K + k], 1), :],
                                     buf.at[k, pl.ds(r, 1), :], sem)

    def start(r, c):
        for k in range(TOP_K):
            copy(r, k).start()
        return c

    def wait(r, c):
        for k in range(TOP_K):
            copy(r, k).wait()
        return c

    lax.fori_loop(0, COMBINE_TILE, start, 0)
    lax.fori_loop(0, COMBINE_TILE, wait, 0)
    gate = gate_ref[...]
    y = gate[:, 0:1] * buf[0]
    for k in range(1, TOP_K):
        y = y + gate[:, k:k + 1] * buf[k]
    out_ref[...] = _layer_norm(DEEPNORM_ALPHA * x1_ref[...] + y, lng_ref[...], lnb_ref[...])


def _combine(pos3, gate, x1, ys, lng, lnb):
    T = x1.shape[0]
    row = lambda w: pl.BlockSpec((COMBINE_TILE, w), lambda i: (i, 0))
    full = lambda s: pl.BlockSpec(s, lambda i: (0, 0))
    return pl.pallas_call(
        _combine_kernel,
        grid=(T // COMBINE_TILE,),
        in_specs=[pl.BlockSpec((1, 1, COMBINE_TILE * TOP_K), lambda i: (i, 0, 0), memory_space=pltpu.SMEM),
                  row(LANES), row(D_MODEL), pl.BlockSpec(memory_space=pl.ANY),
                  full((1, D_MODEL)), full((1, D_MODEL))],
        out_specs=row(D_MODEL),
        out_shape=jax.ShapeDtypeStruct((T, D_MODEL), F32),
        scratch_shapes=[pltpu.VMEM((TOP_K, COMBINE_TILE, D_MODEL), F32), pltpu.SemaphoreType.DMA(())],
        compiler_params=_cparams(("arbitrary",)),
        name="combine_ln2",
    )(pos3, gate, x1, ys, lng, lnb)


def _layer(x2, batch, seq, w_in, b_in, sinks, w_attn_br, conv_w, conv_b, conv_ln_g, conv_ln_b,
           w_conv_br, b_conv_br, w_o, ln1_g, ln1_b, w_router, b_router, w_up, b_up, w_down, b_down,
           ln2_g, ln2_b):
    T = x2.shape[0]
    r2 = lambda v: v.reshape(1, -1)
    q, k, v, a, g = _inproj(x2, w_in.astype(BF16), r2(b_in))
    o = _attention(q, k, v, sinks, batch, seq)
    yc = _conv_branch(a, conv_w, r2(conv_b), r2(conv_ln_g), r2(conv_ln_b), batch, seq)

    wr = jnp.pad(w_router, ((0, 0), (0, LANES - N_EXPERTS)))
    wr_hi, wr_lo = _split_bf16(wr)
    br = jnp.pad(b_router, (0, LANES - N_EXPERTS), constant_values=NEG_BIG)
    x1, logits = _merge(o, yc, g, x2, w_attn_br.astype(BF16), w_conv_br.astype(BF16), r2(b_conv_br),
                        w_o.astype(BF16), r2(ln1_g), r2(ln1_b), wr_hi, wr_lo, r2(br))

    idx, gate, rank, cnt = _route(logits)
    counts = cnt[0, :N_EXPERTS]
    padded = (counts + EXPERT_TILE - 1) // EXPERT_TILE * EXPERT_TILE
    pad_end = jnp.cumsum(padded)
    pad_start = pad_end - padded
    pos = pad_start[idx[:, :TOP_K]] + rank[:, :TOP_K]
    n_slots = T * TOP_K + N_EXPERTS * EXPERT_TILE
    n_tiles = n_slots // EXPERT_TILE
    tok = jnp.broadcast_to(jnp.arange(T, dtype=jnp.int32)[:, None], (T, TOP_K))
    slot_tok = jnp.zeros((n_slots,), jnp.int32).at[pos.reshape(-1)].set(tok.reshape(-1), unique_indices=True)
    tile_expert = jnp.minimum(
        jnp.searchsorted(pad_end, jnp.arange(n_tiles, dtype=jnp.int32) * EXPERT_TILE, side='right'),
        N_EXPERTS - 1).astype(jnp.int32)
    n_used = (pad_end[-1:] // EXPERT_TILE).astype(jnp.int32)

    wup = jnp.concatenate([w_up[..., 0::2], w_up[..., 1::2]], axis=-1).astype(BF16)
    bup = jnp.concatenate([b_up[..., 0::2], b_up[..., 1::2]], axis=-1)[:, None, :]
    ys = _experts(tile_expert, n_used, slot_tok.reshape(n_tiles, 1, EXPERT_TILE), x1, wup, bup,
                  w_down.astype(BF16), b_down[:, None, :])

    pos3 = pos.reshape(T // COMBINE_TILE, 1, COMBINE_TILE * TOP_K)
    return _combine(pos3, gate, x1, ys, r2(ln2_g), r2(ln2_b))


def kernel(x, w_in, b_in, attn_sinks, w_attn_br, conv_w, conv_b, conv_ln_g, conv_ln_b, w_conv_br,
           b_conv_br, w_o, ln1_g, ln1_b, w_router, b_router, w_up, b_up, w_down, b_down, ln2_g, ln2_b):
    B, S, D = x.shape
    x2 = x.reshape(B * S, D)
    for l in range(DEPTH):
        x2 = _layer(x2, B, S, w_in[l], b_in[l], attn_sinks[l], w_attn_br[l], conv_w[l], conv_b[l],
                    conv_ln_g[l], conv_ln_b[l], w_conv_br[l], b_conv_br[l], w_o[l], ln1_g[l], ln1_b[l],
                    w_router[l], b_router[l], w_up[l], b_up[l], w_down[l], b_down[l], ln2_g[l], ln2_b[l])
    return x2.reshape(B, S, D)
```

```python
import functools

import jax
import jax.numpy as jnp
from jax import lax
from jax.experimental import pallas as pl
from jax.experimental.pallas import tpu as pltpu

F32 = jnp.float32
BF16 = jnp.bfloat16

D_MODEL = 1024
N_HEADS = 8
N_KV_HEADS = 2
HEAD_DIM = 64
ATTN_WIDTH = N_HEADS * HEAD_DIM
KV_WIDTH = N_KV_HEADS * HEAD_DIM
WINDOW = 128
ATTN_BLOCK = 128
CONV_CH = 512
CONV_WIDTH = 31
Q_END = ATTN_WIDTH
K_END = Q_END + KV_WIDTH
V_END = K_END + KV_WIDTH
CONV_END = V_END + 2 * CONV_CH
IN_WIDTH = CONV_END + 2 * D_MODEL
N_EXPERTS = 32
TOP_K = 4
D_EXPERT = D_MODEL
SWIGLU_ALPHA = 1.702
SWIGLU_LIMIT = 7.0
LN_EPS = 1e-5
DEPTH = 1
DEEPNORM_ALPHA = (2 * DEPTH) ** 0.25

LANES = 128
NEG_BIG = -1e30
TOKEN_TILE = 512
CONV_HALO = 32
EXPERT_TILE = 256
COMBINE_TILE = 256
CAST_ROWS = 64
VMEM_LIMIT = 56 * 1024 * 1024


def _layer_norm(z, g, b):
    mu = jnp.mean(z, axis=-1, keepdims=True)
    zc = z - mu
    var = jnp.mean(zc * zc, axis=-1, keepdims=True)
    return zc * lax.rsqrt(var + LN_EPS) * g + b


def _cparams(sem):
    return pltpu.CompilerParams(dimension_semantics=sem, vmem_limit_bytes=VMEM_LIMIT)


def _inproj_kernel(x_ref, w_ref, b_ref, q_ref, k_ref, v_ref, a_ref, g_ref):
    x = x_ref[...].astype(BF16)

    def proj(lo, hi):
        return jnp.dot(x, w_ref[:, lo:hi], preferred_element_type=F32) + b_ref[:, lo:hi]

    qkv = proj(0, V_END)
    q_ref[...] = qkv[:, :Q_END].astype(BF16)
    k_ref[...] = qkv[:, Q_END:K_END].astype(BF16)
    v_ref[...] = qkv[:, K_END:V_END].astype(BF16)
    c = proj(V_END, CONV_END)
    a_ref[...] = c[:, :CONV_CH] * jax.nn.sigmoid(c[:, CONV_CH:])
    g_ref[...] = jax.nn.sigmoid(proj(CONV_END, IN_WIDTH))


def _inproj(x2, w_in, b_in):
    T = x2.shape[0]
    row = lambda w: pl.BlockSpec((TOKEN_TILE, w), lambda i: (i, 0))
    full = lambda s: pl.BlockSpec(s, lambda i: (0, 0))
    return pl.pallas_call(
        _inproj_kernel,
        grid=(T // TOKEN_TILE,),
        in_specs=[row(D_MODEL), full((D_MODEL, IN_WIDTH)), full((1, IN_WIDTH))],
        out_specs=[row(ATTN_WIDTH), row(KV_WIDTH), row(KV_WIDTH), row(CONV_CH), row(2 * D_MODEL)],
        out_shape=[jax.ShapeDtypeStruct((T, ATTN_WIDTH), BF16),
                   jax.ShapeDtypeStruct((T, KV_WIDTH), BF16),
                   jax.ShapeDtypeStruct((T, KV_WIDTH), BF16),
                   jax.ShapeDtypeStruct((T, CONV_CH), F32),
                   jax.ShapeDtypeStruct((T, 2 * D_MODEL), F32)],
        compiler_params=_cparams(("parallel",)),
        name="inproj",
    )(x2, w_in, b_in)


def _attn_kernel(sink_ref, q_ref, kc_ref, kp_ref, vc_ref, vp_ref, o_ref):
    i = pl.program_id(1)
    q = q_ref[...]
    kk = jnp.concatenate([kp_ref[...], kc_ref[...]], axis=0)
    vv = jnp.concatenate([vp_ref[...], vc_ref[...]], axis=0)
    qi = lax.broadcasted_iota(jnp.int32, (ATTN_BLOCK, 2 * ATTN_BLOCK), 0)
    kj = lax.broadcasted_iota(jnp.int32, (ATTN_BLOCK, 2 * ATTN_BLOCK), 1)
    delta = qi + ATTN_BLOCK - kj
    mask = (delta >= 0) & (delta < WINDOW) & ((kj >= ATTN_BLOCK) | (i > 0))
    group = N_HEADS // N_KV_HEADS
    outs = []
    for h in range(N_HEADS):
        g = h // group
        qh = q[:, h * HEAD_DIM:(h + 1) * HEAD_DIM]
        kh = kk[:, g * HEAD_DIM:(g + 1) * HEAD_DIM]
        vh = vv[:, g * HEAD_DIM:(g + 1) * HEAD_DIM]
        s = lax.dot_general(qh, kh, (((1,), (1,)), ((), ())),
                            preferred_element_type=F32) * (HEAD_DIM ** -0.5)
        s = jnp.where(mask, s, NEG_BIG)
        sink = sink_ref[h]
        m = jnp.maximum(jnp.max(s, axis=-1, keepdims=True), sink)
        p = jnp.exp(s - m)
        denom = jnp.sum(p, axis=-1, keepdims=True) + jnp.exp(sink - m)
        p = p / denom
        outs.append(jnp.dot(p.astype(BF16), vh, preferred_element_type=F32))
    o_ref[...] = jnp.concatenate(outs, axis=1).astype(BF16)


def _attention(q, k, v, sinks, batch, seq):
    T = q.shape[0]
    nb = seq // ATTN_BLOCK
    cur = lambda w: pl.BlockSpec((ATTN_BLOCK, w), lambda b, i: (b * nb + i, 0))
    prev = lambda w: pl.BlockSpec((ATTN_BLOCK, w), lambda b, i: (b * nb + jnp.maximum(i - 1, 0), 0))
    return pl.pallas_call(
        _attn_kernel,
        grid=(batch, nb),
        in_specs=[pl.BlockSpec(memory_space=pltpu.SMEM),
                  cur(ATTN_WIDTH), cur(KV_WIDTH), prev(KV_WIDTH), cur(KV_WIDTH), prev(KV_WIDTH)],
        out_specs=cur(ATTN_WIDTH),
        out_shape=jax.ShapeDtypeStruct((T, ATTN_WIDTH), BF16),
        compiler_params=_cparams(("parallel", "parallel")),
        name="swa_attention",
    )(sinks, q, k, k, v, v)


def _conv_kernel(ap_ref, ac_ref, w_ref, cb_ref, g_ref, b_ref, o_ref, ext_ref):
    i = pl.program_id(1)
    ext_ref[0:CONV_HALO, :] = jnp.where(i > 0, ap_ref[...], 0.0)
    ext_ref[CONV_HALO:, :] = ac_ref[...]
    acc = jnp.zeros((TOKEN_TILE, CONV_CH), F32) + cb_ref[...]
    base = CONV_HALO - (CONV_WIDTH - 1)
    for j in range(CONV_WIDTH):
        acc = acc + ext_ref[base + j:base + j + TOKEN_TILE, :] * w_ref[j:j + 1, :]
    y = _layer_norm(acc, g_ref[...], b_ref[...])
    o_ref[...] = (y * jax.nn.sigmoid(y)).astype(BF16)


def _conv_branch(a, conv_w, conv_b, ln_g, ln_b, batch, seq):
    T = a.shape[0]
    nt = seq // TOKEN_TILE
    halo_per_tile = TOKEN_TILE // CONV_HALO
    cur = pl.BlockSpec((TOKEN_TILE, CONV_CH), lambda b, i: (b * nt + i, 0))
    prev = pl.BlockSpec((CONV_HALO, CONV_CH),
                        lambda b, i: (jnp.maximum((b * nt + i) * halo_per_tile - 1, 0), 0))
    full = lambda s: pl.BlockSpec(s, lambda b, i: (0, 0))
    return pl.pallas_call(
        _conv_kernel,
        grid=(batch, nt),
        in_specs=[prev, cur, full((CONV_WIDTH, CONV_CH)), full((1, CONV_CH)),
                  full((1, CONV_CH)), full((1, CONV_CH))],
        out_specs=cur,
        out_shape=jax.ShapeDtypeStruct((T, CONV_CH), BF16),
        scratch_shapes=[pltpu.VMEM((CONV_HALO + TOKEN_TILE, CONV_CH), F32)],
        compiler_params=_cparams(("parallel", "parallel")),
        name="conformer_conv",
    )(a, a, conv_w, conv_b, ln_g, ln_b)


def _split_bf16(x):
    hi = x.astype(BF16)
    lo = (x - hi.astype(F32)).astype(BF16)
    return hi, lo


def _merge_kernel(o_ref, yc_ref, g_ref, x_ref, wa_ref, wc_ref, bc_ref, wo_ref, lng_ref, lnb_ref,
                  wrh_ref, wrl_ref, br_ref, x1_ref, lg_ref):
    ya = jnp.dot(o_ref[...], wa_ref[...], preferred_element_type=F32)
    yc = jnp.dot(yc_ref[...], wc_ref[...], preferred_element_type=F32) + bc_ref[...]
    merged = g_ref[:, :D_MODEL] * ya + g_ref[:, D_MODEL:] * yc
    z = DEEPNORM_ALPHA * x_ref[...] + jnp.dot(merged.astype(BF16), wo_ref[...],
                                              preferred_element_type=F32)
    x1 = _layer_norm(z, lng_ref[...], lnb_ref[...])
    x1_ref[...] = x1
    hi, lo = _split_bf16(x1)
    wh = wrh_ref[...]
    lg = (jnp.dot(hi, wh, preferred_element_type=F32)
          + (jnp.dot(hi, wrl_ref[...], preferred_element_type=F32)
             + jnp.dot(lo, wh, preferred_element_type=F32)))
    lg_ref[...] = lg + br_ref[...]


def _merge(o, yc, g, x2, wa, wc, bc, wo, lng, lnb, wrh, wrl, br):
    T = x2.shape[0]
    row = lambda w: pl.BlockSpec((TOKEN_TILE, w), lambda i: (i, 0))
    full = lambda s: pl.BlockSpec(s, lambda i: (0, 0))
    return pl.pallas_call(
        _merge_kernel,
        grid=(T // TOKEN_TILE,),
        in_specs=[row(ATTN_WIDTH), row(CONV_CH), row(2 * D_MODEL), row(D_MODEL),
                  full((ATTN_WIDTH, D_MODEL)), full((CONV_CH, D_MODEL)), full((1, D_MODEL)),
                  full((D_MODEL, D_MODEL)), full((1, D_MODEL)), full((1, D_MODEL)),
                  full((D_MODEL, LANES)), full((D_MODEL, LANES)), full((1, LANES))],
        out_specs=[row(D_MODEL), row(LANES)],
        out_shape=[jax.ShapeDtypeStruct((T, D_MODEL), F32), jax.ShapeDtypeStruct((T, LANES), F32)],
        compiler_params=_cparams(("parallel",)),
        name="merge_ln1_router",
    )(o, yc, g, x2, wa, wc, bc, wo, lng, lnb, wrh, wrl, br)


def _route_kernel(lg_ref, idx_ref, gate_ref, rank_ref, cnt_ref, carry_ref):
    @pl.when(pl.program_id(0) == 0)
    def _():
        carry_ref[...] = jnp.zeros_like(carry_ref)

    work = lg_ref[...]
    shape = work.shape
    lane_i = lax.broadcasted_iota(jnp.int32, shape, 1)
    lane = lane_i.astype(F32)
    vals, hots, firsts = [], [], []
    for _ in range(TOP_K):
        m = jnp.max(work, axis=-1, keepdims=True)
        first = jnp.min(jnp.where(work == m, lane, float(LANES)), axis=-1, keepdims=True)
        hot = lane == first
        vals.append(m)
        hots.append(hot)
        firsts.append(first)
        work = jnp.where(hot, -jnp.inf, work)
    exps = [jnp.exp(v - vals[0]) for v in vals]
    denom = exps[0] + exps[1] + exps[2] + exps[3]
    sel = hots[0] | hots[1] | hots[2] | hots[3]
    sel_f = jnp.where(sel, 1.0, 0.0)
    row = lax.broadcasted_iota(jnp.int32, (shape[0], shape[0]), 0)
    col = lax.broadcasted_iota(jnp.int32, (shape[0], shape[0]), 1)
    earlier = jnp.where(row > col, 1.0, 0.0).astype(BF16)
    prefix = jnp.dot(earlier, sel_f.astype(BF16), preferred_element_type=F32) + carry_ref[...]
    idx = jnp.zeros(shape, F32)
    gate = jnp.zeros(shape, F32)
    rank = jnp.zeros(shape, F32)
    for k in range(TOP_K):
        here = lane_i == k
        rank_k = jnp.sum(jnp.where(hots[k], prefix, 0.0), axis=-1, keepdims=True)
        idx = jnp.where(here, firsts[k], idx)
        gate = jnp.where(here, exps[k] / denom, gate)
        rank = jnp.where(here, rank_k, rank)
    idx_ref[...] = idx.astype(jnp.int32)
    gate_ref[...] = gate
    rank_ref[...] = rank.astype(jnp.int32)
    carry_ref[...] = carry_ref[...] + jnp.sum(sel_f, axis=0, keepdims=True)
    cnt_ref[...] = carry_ref[...].astype(jnp.int32)


def _route(logits):
    T = logits.shape[0]
    row = pl.BlockSpec((TOKEN_TILE, LANES), lambda i: (i, 0))
    return pl.pallas_call(
        _route_kernel,
        grid=(T // TOKEN_TILE,),
        in_specs=[row],
        out_specs=[row, row, row, pl.BlockSpec((1, LANES), lambda i: (0, 0))],
        out_shape=[jax.ShapeDtypeStruct((T, LANES), jnp.int32),
                   jax.ShapeDtypeStruct((T, LANES), F32),
                   jax.ShapeDtypeStruct((T, LANES), jnp.int32),
                   jax.ShapeDtypeStruct((1, LANES), jnp.int32)],
        scratch_shapes=[pltpu.VMEM((1, LANES), F32)],
        compiler_params=_cparams(("arbitrary",)),
        name="route_topk_rank",
    )(logits)


def _row_copy(src, dst, sem, src_row, dst_row):
    return pltpu.make_async_copy(src.at[pl.ds(src_row, 1), :], dst.at[pl.ds(dst_row, 1), :], sem)


def _expert_kernel(te_ref, first_ref, nxt_ref, par_ref, nu_ref,
                   tok0_ref, tokn_ref, dst_ref, x1_hbm, wup_hbm, wdn_hbm, bup_ref, bdn_ref,
                   out_hbm,
                   wup_f32, wup_bf, wdn_bf, xbuf, ybuf, wsem, gsem, ssem):
    s = pl.program_id(0)
    n_used = nu_ref[0]
    slot = lax.rem(s, 2)
    wslot = par_ref[s]

    def weight_copies(e, w):
        return (pltpu.make_async_copy(wup_hbm.at[e], wup_f32, wsem.at[0]),
                pltpu.make_async_copy(wdn_hbm.at[e], wdn_bf.at[w], wsem.at[1]))

    def gather(tok_ref, buf_slot, start):
        def body(r, c):
            cp = _row_copy(x1_hbm, xbuf.at[buf_slot], gsem.at[buf_slot], tok_ref[0, 0, r], r)
            cp.start() if start else cp.wait()
            return c
        lax.fori_loop(0, EXPERT_TILE, body, 0, unroll=8)

    def scatter(buf_slot, start):
        def body(r, c):
            cp = _row_copy(ybuf.at[buf_slot], out_hbm, ssem.at[buf_slot], r, dst_ref[0, 0, r])
            cp.start() if start else cp.wait()
            return c
        lax.fori_loop(0, EXPERT_TILE, body, 0, unroll=8)

    @pl.when(s == 0)
    def _():
        for cp in weight_copies(te_ref[0], par_ref[0]):
            cp.start()
        gather(tok0_ref, 0, True)
        ybuf[...] = jnp.zeros_like(ybuf)
        spare0 = out_hbm.shape[0] - 2 * EXPERT_TILE
        fills = [pltpu.make_async_copy(ybuf.at[b], out_hbm.at[pl.ds(spare0 + b * EXPERT_TILE, EXPERT_TILE), :],
                                       ssem.at[b]) for b in range(2)]
        for cp in fills:
            cp.start()
        for cp in fills:
            cp.wait()

    @pl.when(s < n_used)
    def _():
        @pl.when(first_ref[s] == 1)
        def _():
            for cp in weight_copies(te_ref[s], wslot):
                cp.wait()

            def cast_rows(i, c):
                r = pl.multiple_of(i * CAST_ROWS, CAST_ROWS)
                wup_bf[pl.ds(r, CAST_ROWS), :] = wup_f32[pl.ds(r, CAST_ROWS), :].astype(BF16)
                return c

            lax.fori_loop(0, D_MODEL // CAST_ROWS, cast_rows, 0)

            @pl.when(nxt_ref[s] >= 0)
            def _():
                for cp in weight_copies(nxt_ref[s], 1 - wslot):
                    cp.start()

        gather(tok0_ref, slot, False)

        @pl.when(s + 1 < n_used)
        def _():
            gather(tokn_ref, 1 - slot, True)

        x = xbuf[slot].astype(BF16)
        up = jnp.dot(x, wup_bf[...], preferred_element_type=F32) + bup_ref[0]
        even = (lax.broadcasted_iota(jnp.int32, (EXPERT_TILE, LANES), 1) & 1) == 0
        acts = []
        for c in range(D_EXPERT // LANES):
            a = up[:, 2 * c * LANES:(2 * c + 1) * LANES]
            b = up[:, (2 * c + 1) * LANES:(2 * c + 2) * LANES]
            glu = jnp.where(even, a, pltpu.roll(b, 1, axis=1))
            lin = jnp.where(even, pltpu.roll(a, LANES - 1, axis=1), b)
            glu = jnp.minimum(glu, SWIGLU_LIMIT)
            lin = jnp.clip(lin, -SWIGLU_LIMIT, SWIGLU_LIMIT)
            acts.append((glu * jax.nn.sigmoid(SWIGLU_ALPHA * glu) * (lin + 1.0)).astype(BF16))
        act = jnp.concatenate(acts, axis=1)
        ybuf[slot] = jnp.dot(act, wdn_bf[wslot], preferred_element_type=F32) + bdn_ref[0]

        @pl.when(s >= 1)
        def _():
            scatter(1 - slot, False)

        scatter(slot, True)

        @pl.when(s == n_used - 1)
        def _():
            scatter(slot, False)


def _experts(tile_expert, tile_first, tile_next, tile_par, n_used, slot_tok3, slot_dst3, x1, w_up, w_down,
             bup, bdn, n_out_rows):
    n_tiles = slot_tok3.shape[0]
    smem_tile = lambda f: pl.BlockSpec((1, 1, EXPERT_TILE), f, memory_space=pltpu.SMEM)
    by_expert = lambda w: pl.BlockSpec((1, 1, w), lambda s, te, *_: (te[s], 0, 0))
    grid_spec = pltpu.PrefetchScalarGridSpec(
        num_scalar_prefetch=5,
        grid=(n_tiles,),
        in_specs=[smem_tile(lambda s, *_: (s, 0, 0)),
                  smem_tile(lambda s, *_: (jnp.minimum(s + 1, n_tiles - 1), 0, 0)),
                  smem_tile(lambda s, *_: (s, 0, 0)),
                  pl.BlockSpec(memory_space=pl.ANY),
                  pl.BlockSpec(memory_space=pl.ANY),
                  pl.BlockSpec(memory_space=pl.ANY),
                  by_expert(2 * D_EXPERT), by_expert(D_MODEL)],
        out_specs=pl.BlockSpec(memory_space=pl.ANY),
        scratch_shapes=[pltpu.VMEM((D_MODEL, 2 * D_EXPERT), F32),
                        pltpu.VMEM((D_MODEL, 2 * D_EXPERT), BF16),
                        pltpu.VMEM((2, D_EXPERT, D_MODEL), BF16),
                        pltpu.VMEM((2, EXPERT_TILE, D_MODEL), F32),
                        pltpu.VMEM((2, EXPERT_TILE, D_MODEL), F32),
                        pltpu.SemaphoreType.DMA((2,)),
                        pltpu.SemaphoreType.DMA((2,)),
                        pltpu.SemaphoreType.DMA((2,))],
    )
    return pl.pallas_call(
        _expert_kernel,
        grid_spec=grid_spec,
        out_shape=jax.ShapeDtypeStruct((n_out_rows, D_MODEL), F32),
        compiler_params=_cparams(("arbitrary",)),
        name="expert_ffn",
    )(tile_expert, tile_first, tile_next, tile_par, n_used, slot_tok3, slot_tok3, slot_dst3, x1, w_up, w_down,
      bup, bdn)


def _combine_kernel(gate_ref, x1_ref, y0_ref, y1_ref, y2_ref, y3_ref, lng_ref, lnb_ref, out_ref):
    gate = gate_ref[...]
    y = gate[:, 0:1] * y0_ref[...]
    for k, y_ref in ((1, y1_ref), (2, y2_ref), (3, y3_ref)):
        y = y + gate[:, k:k + 1] * y_ref[...]
    out_ref[...] = _layer_norm(DEEPNORM_ALPHA * x1_ref[...] + y, lng_ref[...], lnb_ref[...])


def _combine(gate, x1, out4, lng, lnb):
    T = x1.shape[0]
    nt = T // COMBINE_TILE
    row = lambda w: pl.BlockSpec((COMBINE_TILE, w), lambda i: (i, 0))
    choice = lambda k: pl.BlockSpec((COMBINE_TILE, D_MODEL), lambda i: (k * nt + i, 0))
    full = lambda s: pl.BlockSpec(s, lambda i: (0, 0))
    return pl.pallas_call(
        _combine_kernel,
        grid=(nt,),
        in_specs=[row(LANES), row(D_MODEL), choice(0), choice(1), choice(2), choice(3),
                  full((1, D_MODEL)), full((1, D_MODEL))],
        out_specs=row(D_MODEL),
        out_shape=jax.ShapeDtypeStruct((T, D_MODEL), F32),
        compiler_params=_cparams(("parallel",)),
        name="combine_ln2",
    )(gate, x1, out4, out4, out4, out4, lng, lnb)


def _lookup(table, idx):
    hit = idx[..., None] == jnp.arange(table.shape[0], dtype=idx.dtype)
    return jnp.sum(jnp.where(hit, table, 0), axis=-1)


def _layer(x2, batch, seq, w_in, b_in, sinks, w_attn_br, conv_w, conv_b, conv_ln_g, conv_ln_b,
           w_conv_br, b_conv_br, w_o, ln1_g, ln1_b, w_router, b_router, w_up, b_up, w_down, b_down,
           ln2_g, ln2_b):
    T = x2.shape[0]
    r2 = lambda v: v.reshape(1, -1)
    q, k, v, a, g = _inproj(x2, w_in.astype(BF16), r2(b_in))
    o = _attention(q, k, v, sinks, batch, seq)
    yc = _conv_branch(a, conv_w, r2(conv_b), r2(conv_ln_g), r2(conv_ln_b), batch, seq)

    wr = jnp.pad(w_router, ((0, 0), (0, LANES - N_EXPERTS)))
    wr_hi, wr_lo = _split_bf16(wr)
    br = jnp.pad(b_router, (0, LANES - N_EXPERTS), constant_values=NEG_BIG)
    x1, logits = _merge(o, yc, g, x2, w_attn_br.astype(BF16), w_conv_br.astype(BF16), r2(b_conv_br),
                        w_o.astype(BF16), r2(ln1_g), r2(ln1_b), wr_hi, wr_lo, r2(br))

    i32 = jnp.int32
    idx, gate, rank, cnt = _route(logits)
    n_assign = T * TOP_K
    n_pad = N_EXPERTS * EXPERT_TILE
    n_slots = n_assign + n_pad
    n_tiles = n_slots // EXPERT_TILE
    counts = cnt[0, :N_EXPERTS]
    padded = (counts + EXPERT_TILE - 1) // EXPERT_TILE * EXPERT_TILE
    pad_end = jnp.cumsum(padded)
    pad_start = pad_end - padded
    pos = _lookup(pad_start, idx[:, :TOP_K]) + rank[:, :TOP_K]
    pad_cnt = padded - counts
    pad_cum = jnp.cumsum(pad_cnt)
    d = jnp.arange(n_pad, dtype=i32)
    e_d = jnp.minimum(jnp.sum(pad_cum[None, :] <= d[:, None], axis=1), N_EXPERTS - 1).astype(i32)
    key_in = _lookup(pad_start + counts - (pad_cum - pad_cnt), e_d) + d
    key_tail = pad_end[-1] + (d - pad_cum[-1])
    pad_keys = jnp.where(d < pad_cum[-1], key_in, key_tail)
    keys = jnp.concatenate([pos.reshape(-1), pad_keys]).astype(i32)
    vals = jnp.concatenate([jnp.arange(n_assign, dtype=i32), jnp.full((n_pad,), -1, i32)])
    _, slot_a = lax.sort_key_val(keys, vals)
    slot_i = jnp.arange(n_slots, dtype=i32)
    real = slot_a >= 0
    slot_tok = jnp.where(real, slot_a // TOP_K, 0)
    slot_dst = jnp.where(real, (slot_a % TOP_K) * T + slot_a // TOP_K, n_assign + slot_i % (2 * EXPERT_TILE))

    tile_start = jnp.arange(n_tiles, dtype=i32) * EXPERT_TILE
    tile_expert = jnp.minimum(jnp.sum(pad_end[None, :] <= tile_start[:, None], axis=1), N_EXPERTS - 1).astype(i32)
    n_used = (pad_end[-1] // EXPERT_TILE).astype(i32)
    tile_first = (tile_start == _lookup(pad_start, tile_expert)).astype(i32)
    group_end = _lookup(pad_end, tile_expert) // EXPERT_TILE
    tile_next = jnp.where(group_end < n_used,
                          jnp.take(tile_expert, jnp.minimum(group_end, n_tiles - 1)), -1).astype(i32)

    tile_par = ((jnp.cumsum(tile_first) - 1) % 2).astype(i32)
    half = LANES // 2
    wdn = w_down.reshape(N_EXPERTS, D_EXPERT // LANES, 2, half, D_MODEL).transpose(0, 1, 3, 2, 4)
    wdn = wdn.reshape(N_EXPERTS, D_EXPERT, D_MODEL).astype(BF16)
    out4 = _experts(tile_expert, tile_first, tile_next, tile_par, n_used.reshape(1),
                    slot_tok.reshape(n_tiles, 1, EXPERT_TILE), slot_dst.reshape(n_tiles, 1, EXPERT_TILE),
                    x1, w_up, wdn, b_up[:, None, :], b_down[:, None, :], n_assign + 2 * EXPERT_TILE)
    return _combine(gate, x1, out4, r2(ln2_g), r2(ln2_b))


def kernel(x, w_in, b_in, attn_sinks, w_attn_br, conv_w, conv_b, conv_ln_g, conv_ln_b, w_conv_br,
           b_conv_br, w_o, ln1_g, ln1_b, w_router, b_router, w_up, b_up, w_down, b_down, ln2_g, ln2_b):
    B, S, D = x.shape
    x2 = x.reshape(B * S, D)
    for l in range(DEPTH):
        x2 = _layer(x2, B, S, w_in[l], b_in[l], attn_sinks[l], w_attn_br[l], conv_w[l], conv_b[l],
                    conv_ln_g[l], conv_ln_b[l], w_conv_br[l], b_conv_br[l], w_o[l], ln1_g[l], ln1_b[l],
                    w_router[l], b_router[l], w_up[l], b_up[l], w_down[l], b_down[l], ln2_g[l], ln2_b[l])
    return x2.reshape(B, S, D)
```

```python
import functools

import jax
import jax.numpy as jnp
from jax import lax
from jax.experimental import pallas as pl
from jax.experimental.pallas import tpu as pltpu

F32 = jnp.float32
BF16 = jnp.bfloat16

D_MODEL = 1024
N_HEADS = 8
N_KV_HEADS = 2
HEAD_DIM = 64
ATTN_WIDTH = N_HEADS * HEAD_DIM
KV_WIDTH = N_KV_HEADS * HEAD_DIM
WINDOW = 128
ATTN_BLOCK = 128
CONV_CH = 512
CONV_WIDTH = 31
Q_END = ATTN_WIDTH
K_END = Q_END + KV_WIDTH
V_END = K_END + KV_WIDTH
CONV_END = V_END + 2 * CONV_CH
IN_WIDTH = CONV_END + 2 * D_MODEL
N_EXPERTS = 32
TOP_K = 4
D_EXPERT = D_MODEL
SWIGLU_ALPHA = 1.702
SWIGLU_LIMIT = 7.0
LN_EPS = 1e-5
DEPTH = 1
DEEPNORM_ALPHA = (2 * DEPTH) ** 0.25

LANES = 128
SUBLANES = 8
NEG_BIG = -1e30
TOKEN_TILE = 512
CONV_HALO = 32
EXPERT_TILE = 256
COMBINE_TILE = 256
CAST_ROWS = 64
DMA_UNROLL = 8
VMEM_LIMIT = 56 * 1024 * 1024


def _layer_norm(z, g, b):
    mu = jnp.mean(z, axis=-1, keepdims=True)
    zc = z - mu
    var = jnp.mean(zc * zc, axis=-1, keepdims=True)
    return zc * lax.rsqrt(var + LN_EPS) * g + b


def _cparams(sem):
    return pltpu.CompilerParams(dimension_semantics=sem, vmem_limit_bytes=VMEM_LIMIT)


def _inproj_kernel(x_ref, w_ref, b_ref, q_ref, k_ref, v_ref, a_ref, g_ref):
    x = x_ref[...].astype(BF16)

    def proj(lo, hi):
        return jnp.dot(x, w_ref[:, lo:hi], preferred_element_type=F32) + b_ref[:, lo:hi]

    qkv = proj(0, V_END)
    q_ref[...] = qkv[:, :Q_END].astype(BF16)
    k_ref[...] = qkv[:, Q_END:K_END].astype(BF16)
    v_ref[...] = qkv[:, K_END:V_END].astype(BF16)
    c = proj(V_END, CONV_END)
    a_ref[...] = c[:, :CONV_CH] * jax.nn.sigmoid(c[:, CONV_CH:])
    g_ref[...] = jax.nn.sigmoid(proj(CONV_END, IN_WIDTH))


def _inproj(x2, w_in, b_in):
    T = x2.shape[0]
    row = lambda w: pl.BlockSpec((TOKEN_TILE, w), lambda i: (i, 0))
    full = lambda s: pl.BlockSpec(s, lambda i: (0, 0))
    return pl.pallas_call(
        _inproj_kernel,
        grid=(T // TOKEN_TILE,),
        in_specs=[row(D_MODEL), full((D_MODEL, IN_WIDTH)), full((1, IN_WIDTH))],
        out_specs=[row(ATTN_WIDTH), row(KV_WIDTH), row(KV_WIDTH), row(CONV_CH), row(2 * D_MODEL)],
        out_shape=[jax.ShapeDtypeStruct((T, ATTN_WIDTH), BF16),
                   jax.ShapeDtypeStruct((T, KV_WIDTH), BF16),
                   jax.ShapeDtypeStruct((T, KV_WIDTH), BF16),
                   jax.ShapeDtypeStruct((T, CONV_CH), F32),
                   jax.ShapeDtypeStruct((T, 2 * D_MODEL), F32)],
        compiler_params=_cparams(("parallel",)),
        name="inproj",
    )(x2, w_in, b_in)


def _attn_kernel(sink_ref, q_ref, kc_ref, kp_ref, vc_ref, vp_ref, o_ref):
    i = pl.program_id(1)
    q = q_ref[...]
    kk = jnp.concatenate([kp_ref[...], kc_ref[...]], axis=0)
    vv = jnp.concatenate([vp_ref[...], vc_ref[...]], axis=0)
    qi = lax.broadcasted_iota(jnp.int32, (ATTN_BLOCK, 2 * ATTN_BLOCK), 0)
    kj = lax.broadcasted_iota(jnp.int32, (ATTN_BLOCK, 2 * ATTN_BLOCK), 1)
    delta = qi + ATTN_BLOCK - kj
    mask = (delta >= 0) & (delta < WINDOW) & ((kj >= ATTN_BLOCK) | (i > 0))
    group = N_HEADS // N_KV_HEADS
    outs = []
    for h in range(N_HEADS):
        g = h // group
        qh = q[:, h * HEAD_DIM:(h + 1) * HEAD_DIM]
        kh = kk[:, g * HEAD_DIM:(g + 1) * HEAD_DIM]
        vh = vv[:, g * HEAD_DIM:(g + 1) * HEAD_DIM]
        s = lax.dot_general(qh, kh, (((1,), (1,)), ((), ())),
                            preferred_element_type=F32) * (HEAD_DIM ** -0.5)
        s = jnp.where(mask, s, NEG_BIG)
        sink = sink_ref[h]
        m = jnp.maximum(jnp.max(s, axis=-1, keepdims=True), sink)
        p = jnp.exp(s - m)
        denom = jnp.sum(p, axis=-1, keepdims=True) + jnp.exp(sink - m)
        p = p / denom
        outs.append(jnp.dot(p.astype(BF16), vh, preferred_element_type=F32))
    o_ref[...] = jnp.concatenate(outs, axis=1).astype(BF16)


def _attention(q, k, v, sinks, batch, seq):
    T = q.shape[0]
    nb = seq // ATTN_BLOCK
    cur = lambda w: pl.BlockSpec((ATTN_BLOCK, w), lambda b, i: (b * nb + i, 0))
    prev = lambda w: pl.BlockSpec((ATTN_BLOCK, w), lambda b, i: (b * nb + jnp.maximum(i - 1, 0), 0))
    return pl.pallas_call(
        _attn_kernel,
        grid=(batch, nb),
        in_specs=[pl.BlockSpec(memory_space=pltpu.SMEM),
                  cur(ATTN_WIDTH), cur(KV_WIDTH), prev(KV_WIDTH), cur(KV_WIDTH), prev(KV_WIDTH)],
        out_specs=cur(ATTN_WIDTH),
        out_shape=jax.ShapeDtypeStruct((T, ATTN_WIDTH), BF16),
        compiler_params=_cparams(("parallel", "parallel")),
        name="swa_attention",
    )(sinks, q, k, k, v, v)


def _conv_kernel(ap_ref, ac_ref, w_ref, cb_ref, g_ref, b_ref, o_ref, ext_ref):
    i = pl.program_id(1)
    ext_ref[0:CONV_HALO, :] = jnp.where(i > 0, ap_ref[...], 0.0)
    ext_ref[CONV_HALO:, :] = ac_ref[...]
    acc = jnp.zeros((TOKEN_TILE, CONV_CH), F32) + cb_ref[...]
    base = CONV_HALO - (CONV_WIDTH - 1)
    for j in range(CONV_WIDTH):
        acc = acc + ext_ref[base + j:base + j + TOKEN_TILE, :] * w_ref[j:j + 1, :]
    y = _layer_norm(acc, g_ref[...], b_ref[...])
    o_ref[...] = (y * jax.nn.sigmoid(y)).astype(BF16)


def _conv_branch(a, conv_w, conv_b, ln_g, ln_b, batch, seq):
    T = a.shape[0]
    nt = seq // TOKEN_TILE
    halo_per_tile = TOKEN_TILE // CONV_HALO
    cur = pl.BlockSpec((TOKEN_TILE, CONV_CH), lambda b, i: (b * nt + i, 0))
    prev = pl.BlockSpec((CONV_HALO, CONV_CH),
                        lambda b, i: (jnp.maximum((b * nt + i) * halo_per_tile - 1, 0), 0))
    full = lambda s: pl.BlockSpec(s, lambda b, i: (0, 0))
    return pl.pallas_call(
        _conv_kernel,
        grid=(batch, nt),
        in_specs=[prev, cur, full((CONV_WIDTH, CONV_CH)), full((1, CONV_CH)),
                  full((1, CONV_CH)), full((1, CONV_CH))],
        out_specs=cur,
        out_shape=jax.ShapeDtypeStruct((T, CONV_CH), BF16),
        scratch_shapes=[pltpu.VMEM((CONV_HALO + TOKEN_TILE, CONV_CH), F32)],
        compiler_params=_cparams(("parallel", "parallel")),
        name="conformer_conv",
    )(a, a, conv_w, conv_b, ln_g, ln_b)


def _store_token_tiles(ref, x, lead=()):
    rows = x.shape[0]
    for q in range(SUBLANES):
        ref[lead + (pl.ds(q, rows, stride=SUBLANES), slice(None))] = x[:, q * LANES:(q + 1) * LANES]


def _load_token_tiles(ref, rows, lead=()):
    return jnp.concatenate(
        [ref[lead + (pl.ds(q, rows, stride=SUBLANES), slice(None))] for q in range(SUBLANES)], axis=1)


def _split_bf16(x):
    hi = x.astype(BF16)
    lo = (x - hi.astype(F32)).astype(BF16)
    return hi, lo


def _merge_kernel(o_ref, yc_ref, g_ref, x_ref, wa_ref, wc_ref, bc_ref, wo_ref, lng_ref, lnb_ref,
                  wrh_ref, wrl_ref, br_ref, x1_ref, lg_ref):
    ya = jnp.dot(o_ref[...], wa_ref[...], preferred_element_type=F32)
    yc = jnp.dot(yc_ref[...], wc_ref[...], preferred_element_type=F32) + bc_ref[...]
    merged = g_ref[:, :D_MODEL] * ya + g_ref[:, D_MODEL:] * yc
    z = DEEPNORM_ALPHA * x_ref[...] + jnp.dot(merged.astype(BF16), wo_ref[...],
                                              preferred_element_type=F32)
    x1 = _layer_norm(z, lng_ref[...], lnb_ref[...])
    _store_token_tiles(x1_ref, x1)
    hi, lo = _split_bf16(x1)
    wh = wrh_ref[...]
    lg = (jnp.dot(hi, wh, preferred_element_type=F32)
          + (jnp.dot(hi, wrl_ref[...], preferred_element_type=F32)
             + jnp.dot(lo, wh, preferred_element_type=F32)))
    lg_ref[...] = lg + br_ref[...]


def _merge(o, yc, g, x2, wa, wc, bc, wo, lng, lnb, wrh, wrl, br):
    T = x2.shape[0]
    row = lambda w: pl.BlockSpec((TOKEN_TILE, w), lambda i: (i, 0))
    full = lambda s: pl.BlockSpec(s, lambda i: (0, 0))
    return pl.pallas_call(
        _merge_kernel,
        grid=(T // TOKEN_TILE,),
        in_specs=[row(ATTN_WIDTH), row(CONV_CH), row(2 * D_MODEL), row(D_MODEL),
                  full((ATTN_WIDTH, D_MODEL)), full((CONV_CH, D_MODEL)), full((1, D_MODEL)),
                  full((D_MODEL, D_MODEL)), full((1, D_MODEL)), full((1, D_MODEL)),
                  full((D_MODEL, LANES)), full((D_MODEL, LANES)), full((1, LANES))],
        out_specs=[pl.BlockSpec((TOKEN_TILE * SUBLANES, LANES), lambda i: (i, 0)), row(LANES)],
        out_shape=[jax.ShapeDtypeStruct((T * SUBLANES, LANES), F32), jax.ShapeDtypeStruct((T, LANES), F32)],
        compiler_params=_cparams(("parallel",)),
        name="merge_ln1_router",
    )(o, yc, g, x2, wa, wc, bc, wo, lng, lnb, wrh, wrl, br)


def _route_kernel(lg_ref, idx_ref, gate_ref, rank_ref, cnt_ref, carry_ref):
    @pl.when(pl.program_id(0) == 0)
    def _():
        carry_ref[...] = jnp.zeros_like(carry_ref)

    work = lg_ref[...]
    shape = work.shape
    lane_i = lax.broadcasted_iota(jnp.int32, shape, 1)
    lane = lane_i.astype(F32)
    vals, hots, firsts = [], [], []
    for _ in range(TOP_K):
        m = jnp.max(work, axis=-1, keepdims=True)
        first = jnp.min(jnp.where(work == m, lane, float(LANES)), axis=-1, keepdims=True)
        hot = lane == first
        vals.append(m)
        hots.append(hot)
        firsts.append(first)
        work = jnp.where(hot, -jnp.inf, work)
    exps = [jnp.exp(v - vals[0]) for v in vals]
    denom = exps[0] + exps[1] + exps[2] + exps[3]
    sel = hots[0] | hots[1] | hots[2] | hots[3]
    sel_f = jnp.where(sel, 1.0, 0.0)
    row = lax.broadcasted_iota(jnp.int32, (shape[0], shape[0]), 0)
    col = lax.broadcasted_iota(jnp.int32, (shape[0], shape[0]), 1)
    earlier = jnp.where(row > col, 1.0, 0.0).astype(BF16)
    prefix = jnp.dot(earlier, sel_f.astype(BF16), preferred_element_type=F32) + carry_ref[...]
    idx = jnp.zeros(shape, F32)
    gate = jnp.zeros(shape, F32)
    rank = jnp.zeros(shape, F32)
    for k in range(TOP_K):
        here = lane_i == k
        rank_k = jnp.sum(jnp.where(hots[k], prefix, 0.0), axis=-1, keepdims=True)
        idx = jnp.where(here, firsts[k], idx)
        gate = jnp.where(here, exps[k] / denom, gate)
        rank = jnp.where(here, rank_k, rank)
    idx_ref[...] = idx.astype(jnp.int32)
    gate_ref[...] = gate
    rank_ref[...] = rank.astype(jnp.int32)
    carry_ref[...] = carry_ref[...] + jnp.sum(sel_f, axis=0, keepdims=True)
    cnt_ref[...] = carry_ref[...].astype(jnp.int32)


def _route(logits):
    T = logits.shape[0]
    row = pl.BlockSpec((TOKEN_TILE, LANES), lambda i: (i, 0))
    return pl.pallas_call(
        _route_kernel,
        grid=(T // TOKEN_TILE,),
        in_specs=[row],
        out_specs=[row, row, row, pl.BlockSpec((1, LANES), lambda i: (0, 0))],
        out_shape=[jax.ShapeDtypeStruct((T, LANES), jnp.int32),
                   jax.ShapeDtypeStruct((T, LANES), F32),
                   jax.ShapeDtypeStruct((T, LANES), jnp.int32),
                   jax.ShapeDtypeStruct((1, LANES), jnp.int32)],
        scratch_shapes=[pltpu.VMEM((1, LANES), F32)],
        compiler_params=_cparams(("arbitrary",)),
        name="route_topk_rank",
    )(logits)


def _expert_kernel(te_ref, first_ref, nxt_ref, nu_ref,
                   tok0_ref, tokn_ref, dstp_ref, dstc_ref, x1_hbm, wup_hbm, wdn_hbm, bup_ref, bdn_ref,
                   out_hbm,
                   wup_f32, wdn_f32, wdn_perm, wup_bf, wdn_bf, xbuf, ybuf, wsem, gsem, ssem):
    s = pl.program_id(0)
    n_used = nu_ref[0]
    slot = lax.rem(s, 2)
    tile_rows = EXPERT_TILE * SUBLANES
    n_chunks = D_EXPERT // LANES

    def weight_copies(e):
        return (pltpu.make_async_copy(wup_hbm.at[e], wup_f32, wsem.at[0]),
                pltpu.make_async_copy(wdn_hbm.at[e], wdn_f32, wsem.at[1]))

    def token_copy(src, dst, sem, src_tok, dst_tok):
        rows = lambda t: pl.ds(pl.multiple_of(t * SUBLANES, SUBLANES), SUBLANES)
        return pltpu.make_async_copy(src.at[rows(src_tok), :], dst.at[rows(dst_tok), :], sem)

    def wait_tokens(sem):
        def body(g, c):
            for _ in range(DMA_UNROLL):
                token_copy(x1_hbm, xbuf.at[0], sem, 0, 0).wait()
            return c
        lax.fori_loop(0, EXPERT_TILE // DMA_UNROLL, body, 0)

    @pl.when(s == 0)
    def _():
        for cp in weight_copies(te_ref[0]):
            cp.start()

        def first_gather(g, c):
            for j in range(DMA_UNROLL):
                r = g * DMA_UNROLL + j
                token_copy(x1_hbm, xbuf.at[0], gsem.at[0], tok0_ref[0, 0, r], r).start(priority=j % 2)
            return c

        lax.fori_loop(0, EXPERT_TILE // DMA_UNROLL, first_gather, 0)
        ybuf[...] = jnp.zeros_like(ybuf)
        spare0 = out_hbm.shape[0] - 2 * tile_rows
        fills = [pltpu.make_async_copy(ybuf.at[b], out_hbm.at[pl.ds(spare0 + b * tile_rows, tile_rows), :],
                                       ssem.at[b]) for b in range(2)]
        for cp in fills:
            cp.start()
        for cp in fills:
            cp.wait()

    @pl.when(s < n_used)
    def _():
        @pl.when(first_ref[s] == 1)
        def _():
            for cp in weight_copies(te_ref[s]):
                cp.wait()

            def cast_up(i, c):
                r = pl.multiple_of(i * CAST_ROWS, CAST_ROWS)
                wup_bf[pl.ds(r, CAST_ROWS), :] = wup_f32[pl.ds(r, CAST_ROWS), :].astype(BF16)
                return c

            lax.fori_loop(0, D_MODEL // CAST_ROWS, cast_up, 0)

            half = LANES // 2

            def permute_down(c, carry):
                r = pl.multiple_of(c * LANES, LANES)
                for j in range(n_chunks):
                    cols = slice(j * LANES, (j + 1) * LANES)
                    wdn_perm[j, pl.ds(r, half, stride=2), :] = wdn_f32[pl.ds(r, half), cols]
                    wdn_perm[j, pl.ds(r + 1, half, stride=2), :] = wdn_f32[pl.ds(r + half, half), cols]
                return carry

            lax.fori_loop(0, n_chunks, permute_down, 0)

            def cast_down(i, c):
                r = pl.multiple_of(i * CAST_ROWS, CAST_ROWS)
                for j in range(n_chunks):
                    wdn_bf[j, pl.ds(r, CAST_ROWS), :] = wdn_perm[j, pl.ds(r, CAST_ROWS), :].astype(BF16)
                return c

            lax.fori_loop(0, D_EXPERT // CAST_ROWS, cast_down, 0)

            @pl.when(nxt_ref[s] >= 0)
            def _():
                for cp in weight_copies(nxt_ref[s]):
                    cp.start()

        wait_tokens(gsem.at[slot])

        @pl.when(s >= 1)
        def _():
            wait_tokens(ssem.at[slot])

        x = _load_token_tiles(xbuf, EXPERT_TILE, (slot,)).astype(BF16)
        for r in range(EXPERT_TILE):
            token_copy(x1_hbm, xbuf.at[1 - slot], gsem.at[1 - slot], tokn_ref[0, 0, r], r).start(priority=0)
            token_copy(ybuf.at[1 - slot], out_hbm, ssem.at[1 - slot], r, dstp_ref[0, 0, r]).start(priority=1)
        up = jnp.dot(x, wup_bf[...], preferred_element_type=F32) + bup_ref[0]
        even = (lax.broadcasted_iota(jnp.int32, (EXPERT_TILE, LANES), 1) & 1) == 0
        acts = []
        for c in range(D_EXPERT // LANES):
            a = up[:, 2 * c * LANES:(2 * c + 1) * LANES]
            b = up[:, (2 * c + 1) * LANES:(2 * c + 2) * LANES]
            glu = jnp.where(even, a, pltpu.roll(b, 1, axis=1))
            lin = jnp.where(even, pltpu.roll(a, LANES - 1, axis=1), b)
            glu = jnp.minimum(glu, SWIGLU_LIMIT)
            lin = jnp.clip(lin, -SWIGLU_LIMIT, SWIGLU_LIMIT)
            acts.append((glu * jax.nn.sigmoid(SWIGLU_ALPHA * glu) * (lin + 1.0)).astype(BF16))
        act = jnp.concatenate(acts, axis=1)
        wdn = jnp.concatenate([wdn_bf[j] for j in range(n_chunks)], axis=1)
        y = jnp.dot(act, wdn, preferred_element_type=F32) + bdn_ref[0]
        _store_token_tiles(ybuf, y, (slot,))

        @pl.when(s == n_used - 1)
        def _():
            def last_scatter(g, c):
                for j in range(DMA_UNROLL):
                    r = g * DMA_UNROLL + j
                    token_copy(ybuf.at[slot], out_hbm, ssem.at[slot], r, dstc_ref[0, 0, r]).start(priority=j % 2)
                return c

            lax.fori_loop(0, EXPERT_TILE // DMA_UNROLL, last_scatter, 0)
            wait_tokens(ssem.at[1 - slot])
            wait_tokens(ssem.at[slot])
            wait_tokens(gsem.at[1 - slot])


def _experts(tile_expert, tile_first, tile_next, n_used, slot_tok3, slot_dst3, x1t, w_up, w_down,
             bup, bdn, n_out_tokens):
    n_tiles = slot_tok3.shape[0]
    tile_rows = EXPERT_TILE * SUBLANES
    smem_tile = lambda f: pl.BlockSpec((1, 1, EXPERT_TILE), f, memory_space=pltpu.SMEM)
    by_expert = lambda w: pl.BlockSpec((1, 1, w), lambda s, te, *_: (te[s], 0, 0))
    grid_spec = pltpu.PrefetchScalarGridSpec(
        num_scalar_prefetch=4,
        grid=(n_tiles,),
        in_specs=[smem_tile(lambda s, *_: (s, 0, 0)),
                  smem_tile(lambda s, *_: (jnp.minimum(s + 1, n_tiles - 1), 0, 0)),
                  smem_tile(lambda s, *_: (jnp.maximum(s - 1, 0), 0, 0)),
                  smem_tile(lambda s, *_: (s, 0, 0)),
                  pl.BlockSpec(memory_space=pl.ANY),
                  pl.BlockSpec(memory_space=pl.ANY),
                  pl.BlockSpec(memory_space=pl.ANY),
                  by_expert(2 * D_EXPERT), by_expert(D_MODEL)],
        out_specs=pl.BlockSpec(memory_space=pl.ANY),
        scratch_shapes=[pltpu.VMEM((D_MODEL, 2 * D_EXPERT), F32),
                        pltpu.VMEM((D_EXPERT, D_MODEL), F32),
                        pltpu.VMEM((D_MODEL // LANES, D_EXPERT, LANES), F32),
                        pltpu.VMEM((D_MODEL, 2 * D_EXPERT), BF16),
                        pltpu.VMEM((D_MODEL // LANES, D_EXPERT, LANES), BF16),
                        pltpu.VMEM((2, tile_rows, LANES), F32),
                        pltpu.VMEM((2, tile_rows, LANES), F32),
                        pltpu.SemaphoreType.DMA((2,)),
                        pltpu.SemaphoreType.DMA((2,)),
                        pltpu.SemaphoreType.DMA((2,))],
    )
    return pl.pallas_call(
        _expert_kernel,
        grid_spec=grid_spec,
        out_shape=jax.ShapeDtypeStruct((n_out_tokens * SUBLANES, LANES), F32),
        compiler_params=_cparams(("arbitrary",)),
        name="expert_ffn",
    )(tile_expert, tile_first, tile_next, n_used, slot_tok3, slot_tok3, slot_dst3, slot_dst3, x1t, w_up, w_down,
      bup, bdn)


def _combine_kernel(gate_ref, x1_ref, y0_ref, y1_ref, y2_ref, y3_ref, lng_ref, lnb_ref, out_ref):
    gate = gate_ref[...]
    zs = []
    for q in range(SUBLANES):
        rows = pl.ds(q, COMBINE_TILE, stride=SUBLANES)
        y = gate[:, 0:1] * y0_ref[rows, :]
        for k, y_ref in ((1, y1_ref), (2, y2_ref), (3, y3_ref)):
            y = y + gate[:, k:k + 1] * y_ref[rows, :]
        zs.append(DEEPNORM_ALPHA * x1_ref[rows, :] + y)
    out_ref[...] = _layer_norm(jnp.concatenate(zs, axis=1), lng_ref[...], lnb_ref[...])


def _combine(gate, x1t, out4t, lng, lnb):
    T = gate.shape[0]
    nt = T // COMBINE_TILE
    row = lambda w: pl.BlockSpec((COMBINE_TILE, w), lambda i: (i, 0))
    tiles = lambda k: pl.BlockSpec((COMBINE_TILE * SUBLANES, LANES), lambda i: (k * nt + i, 0))
    full = lambda s: pl.BlockSpec(s, lambda i: (0, 0))
    return pl.pallas_call(
        _combine_kernel,
        grid=(nt,),
        in_specs=[row(LANES), tiles(0), tiles(0), tiles(1), tiles(2), tiles(3),
                  full((1, D_MODEL)), full((1, D_MODEL))],
        out_specs=row(D_MODEL),
        out_shape=jax.ShapeDtypeStruct((T, D_MODEL), F32),
        compiler_params=_cparams(("parallel",)),
        name="combine_ln2",
    )(gate, x1t, out4t, out4t, out4t, out4t, lng, lnb)


def _lookup(table, idx):
    hit = idx[..., None] == jnp.arange(table.shape[0], dtype=idx.dtype)
    return jnp.sum(jnp.where(hit, table, 0), axis=-1)


def _layer(x2, batch, seq, w_in, b_in, sinks, w_attn_br, conv_w, conv_b, conv_ln_g, conv_ln_b,
           w_conv_br, b_conv_br, w_o, ln1_g, ln1_b, w_router, b_router, w_up, b_up, w_down, b_down,
           ln2_g, ln2_b):
    T = x2.shape[0]
    r2 = lambda v: v.reshape(1, -1)
    q, k, v, a, g = _inproj(x2, w_in.astype(BF16), r2(b_in))
    o = _attention(q, k, v, sinks, batch, seq)
    yc = _conv_branch(a, conv_w, r2(conv_b), r2(conv_ln_g), r2(conv_ln_b), batch, seq)

    wr = jnp.pad(w_router, ((0, 0), (0, LANES - N_EXPERTS)))
    wr_hi, wr_lo = _split_bf16(wr)
    br = jnp.pad(b_router, (0, LANES - N_EXPERTS), constant_values=NEG_BIG)
    x1t, logits = _merge(o, yc, g, x2, w_attn_br.astype(BF16), w_conv_br.astype(BF16), r2(b_conv_br),
                        w_o.astype(BF16), r2(ln1_g), r2(ln1_b), wr_hi, wr_lo, r2(br))

    i32 = jnp.int32
    idx, gate, rank, cnt = _route(logits)
    n_assign = T * TOP_K
    n_pad = N_EXPERTS * EXPERT_TILE
    n_slots = n_assign + n_pad
    n_tiles = n_slots // EXPERT_TILE
    counts = cnt[0, :N_EXPERTS]
    padded = (counts + EXPERT_TILE - 1) // EXPERT_TILE * EXPERT_TILE
    pad_end = jnp.cumsum(padded)
    pad_start = pad_end - padded
    pos = _lookup(pad_start, idx[:, :TOP_K]) + rank[:, :TOP_K]
    pad_cnt = padded - counts
    pad_cum = jnp.cumsum(pad_cnt)
    d = jnp.arange(n_pad, dtype=i32)
    e_d = jnp.minimum(jnp.sum(pad_cum[None, :] <= d[:, None], axis=1), N_EXPERTS - 1).astype(i32)
    key_in = _lookup(pad_start + counts - (pad_cum - pad_cnt), e_d) + d
    key_tail = pad_end[-1] + (d - pad_cum[-1])
    pad_keys = jnp.where(d < pad_cum[-1], key_in, key_tail)
    keys = jnp.concatenate([pos.reshape(-1), pad_keys]).astype(i32)
    vals = jnp.concatenate([jnp.arange(n_assign, dtype=i32), jnp.full((n_pad,), -1, i32)])
    _, slot_a = lax.sort_key_val(keys, vals)
    slot_i = jnp.arange(n_slots, dtype=i32)
    real = slot_a >= 0
    slot_tok = jnp.where(real, slot_a // TOP_K, 0)
    slot_dst = jnp.where(real, (slot_a % TOP_K) * T + slot_a // TOP_K, n_assign + slot_i % (2 * EXPERT_TILE))

    tile_start = jnp.arange(n_tiles, dtype=i32) * EXPERT_TILE
    tile_expert = jnp.minimum(jnp.sum(pad_end[None, :] <= tile_start[:, None], axis=1), N_EXPERTS - 1).astype(i32)
    n_used = (pad_end[-1] // EXPERT_TILE).astype(i32)
    tile_first = (tile_start == _lookup(pad_start, tile_expert)).astype(i32)
    group_end = _lookup(pad_end, tile_expert) // EXPERT_TILE
    tile_next = jnp.where(group_end < n_used,
                          jnp.take(tile_expert, jnp.minimum(group_end, n_tiles - 1)), -1).astype(i32)

    out4t = _experts(tile_expert, tile_first, tile_next, n_used.reshape(1),
                     slot_tok.reshape(n_tiles, 1, EXPERT_TILE), slot_dst.reshape(n_tiles, 1, EXPERT_TILE),
                     x1t, w_up, w_down, b_up[:, None, :], b_down[:, None, :], n_assign + 2 * EXPERT_TILE)
    return _combine(gate, x1t, out4t, r2(ln2_g), r2(ln2_b))


def kernel(x, w_in, b_in, attn_sinks, w_attn_br, conv_w, conv_b, conv_ln_g, conv_ln_b, w_conv_br,
           b_conv_br, w_o, ln1_g, ln1_b, w_router, b_router, w_up, b_up, w_down, b_down, ln2_g, ln2_b):
    B, S, D = x.shape
    x2 = x.reshape(B * S, D)
    for l in range(DEPTH):
        x2 = _layer(x2, B, S, w_in[l], b_in[l], attn_sinks[l], w_attn_br[l], conv_w[l], conv_b[l],
                    conv_ln_g[l], conv_ln_b[l], w_conv_br[l], b_conv_br[l], w_o[l], ln1_g[l], ln1_b[l],
                    w_router[l], b_router[l], w_up[l], b_up[l], w_down[l], b_down[l], ln2_g[l], ln2_b[l])
    return x2.reshape(B, S, D)
```

```python
import functools

import jax
import jax.numpy as jnp
from jax import lax
from jax.experimental import pallas as pl
from jax.experimental.pallas import tpu as pltpu

F32 = jnp.float32
BF16 = jnp.bfloat16

D_MODEL = 1024
N_HEADS = 8
N_KV_HEADS = 2
HEAD_DIM = 64
ATTN_WIDTH = N_HEADS * HEAD_DIM
KV_WIDTH = N_KV_HEADS * HEAD_DIM
WINDOW = 128
ATTN_BLOCK = 128
CONV_CH = 512
CONV_WIDTH = 31
Q_END = ATTN_WIDTH
K_END = Q_END + KV_WIDTH
V_END = K_END + KV_WIDTH
CONV_END = V_END + 2 * CONV_CH
IN_WIDTH = CONV_END + 2 * D_MODEL
N_EXPERTS = 32
TOP_K = 4
D_EXPERT = D_MODEL
SWIGLU_ALPHA = 1.702
SWIGLU_LIMIT = 7.0
LN_EPS = 1e-5
DEPTH = 1
DEEPNORM_ALPHA = (2 * DEPTH) ** 0.25

LANES = 128
SUBLANES = 8
NEG_BIG = -1e30
TOKEN_TILE = 512
CONV_HALO = 32
EXPERT_TILE = 256
COMBINE_TILE = 256
CAST_ROWS = 64
DMA_UNROLL = 8
RING = 3
VMEM_LIMIT = 56 * 1024 * 1024


def _layer_norm(z, g, b):
    mu = jnp.mean(z, axis=-1, keepdims=True)
    zc = z - mu
    var = jnp.mean(zc * zc, axis=-1, keepdims=True)
    return zc * lax.rsqrt(var + LN_EPS) * g + b


def _cparams(sem):
    return pltpu.CompilerParams(dimension_semantics=sem, vmem_limit_bytes=VMEM_LIMIT)


def _inproj_kernel(x_ref, w_ref, b_ref, q_ref, k_ref, v_ref, a_ref, g_ref):
    x = x_ref[...].astype(BF16)

    def proj(lo, hi):
        return jnp.dot(x, w_ref[:, lo:hi], preferred_element_type=F32) + b_ref[:, lo:hi]

    qkv = proj(0, V_END)
    q_ref[...] = qkv[:, :Q_END].astype(BF16)
    k_ref[...] = qkv[:, Q_END:K_END].astype(BF16)
    v_ref[...] = qkv[:, K_END:V_END].astype(BF16)
    c = proj(V_END, CONV_END)
    a_ref[...] = c[:, :CONV_CH] * jax.nn.sigmoid(c[:, CONV_CH:])
    g_ref[...] = jax.nn.sigmoid(proj(CONV_END, IN_WIDTH))


def _inproj(x2, w_in, b_in):
    T = x2.shape[0]
    row = lambda w: pl.BlockSpec((TOKEN_TILE, w), lambda i: (i, 0))
    full = lambda s: pl.BlockSpec(s, lambda i: (0, 0))
    return pl.pallas_call(
        _inproj_kernel,
        grid=(T // TOKEN_TILE,),
        in_specs=[row(D_MODEL), full((D_MODEL, IN_WIDTH)), full((1, IN_WIDTH))],
        out_specs=[row(ATTN_WIDTH), row(KV_WIDTH), row(KV_WIDTH), row(CONV_CH), row(2 * D_MODEL)],
        out_shape=[jax.ShapeDtypeStruct((T, ATTN_WIDTH), BF16),
                   jax.ShapeDtypeStruct((T, KV_WIDTH), BF16),
                   jax.ShapeDtypeStruct((T, KV_WIDTH), BF16),
                   jax.ShapeDtypeStruct((T, CONV_CH), F32),
                   jax.ShapeDtypeStruct((T, 2 * D_MODEL), F32)],
        compiler_params=_cparams(("parallel",)),
        name="inproj",
    )(x2, w_in, b_in)


def _attn_kernel(sink_ref, q_ref, kc_ref, kp_ref, vc_ref, vp_ref, o_ref):
    i = pl.program_id(1)
    q = q_ref[...]
    kk = jnp.concatenate([kp_ref[...], kc_ref[...]], axis=0)
    vv = jnp.concatenate([vp_ref[...], vc_ref[...]], axis=0)
    qi = lax.broadcasted_iota(jnp.int32, (ATTN_BLOCK, 2 * ATTN_BLOCK), 0)
    kj = lax.broadcasted_iota(jnp.int32, (ATTN_BLOCK, 2 * ATTN_BLOCK), 1)
    delta = qi + ATTN_BLOCK - kj
    mask = (delta >= 0) & (delta < WINDOW) & ((kj >= ATTN_BLOCK) | (i > 0))
    group = N_HEADS // N_KV_HEADS
    outs = []
    for h in range(N_HEADS):
        g = h // group
        qh = q[:, h * HEAD_DIM:(h + 1) * HEAD_DIM]
        kh = kk[:, g * HEAD_DIM:(g + 1) * HEAD_DIM]
        vh = vv[:, g * HEAD_DIM:(g + 1) * HEAD_DIM]
        s = lax.dot_general(qh, kh, (((1,), (1,)), ((), ())),
                            preferred_element_type=F32) * (HEAD_DIM ** -0.5)
        s = jnp.where(mask, s, NEG_BIG)
        sink = sink_ref[h]
        m = jnp.maximum(jnp.max(s, axis=-1, keepdims=True), sink)
        p = jnp.exp(s - m)
        denom = jnp.sum(p, axis=-1, keepdims=True) + jnp.exp(sink - m)
        p = p / denom
        outs.append(jnp.dot(p.astype(BF16), vh, preferred_element_type=F32))
    o_ref[...] = jnp.concatenate(outs, axis=1).astype(BF16)


def _attention(q, k, v, sinks, batch, seq):
    T = q.shape[0]
    nb = seq // ATTN_BLOCK
    cur = lambda w: pl.BlockSpec((ATTN_BLOCK, w), lambda b, i: (b * nb + i, 0))
    prev = lambda w: pl.BlockSpec((ATTN_BLOCK, w), lambda b, i: (b * nb + jnp.maximum(i - 1, 0), 0))
    return pl.pallas_call(
        _attn_kernel,
        grid=(batch, nb),
        in_specs=[pl.BlockSpec(memory_space=pltpu.SMEM),
                  cur(ATTN_WIDTH), cur(KV_WIDTH), prev(KV_WIDTH), cur(KV_WIDTH), prev(KV_WIDTH)],
        out_specs=cur(ATTN_WIDTH),
        out_shape=jax.ShapeDtypeStruct((T, ATTN_WIDTH), BF16),
        compiler_params=_cparams(("parallel", "parallel")),
        name="swa_attention",
    )(sinks, q, k, k, v, v)


def _conv_kernel(ap_ref, ac_ref, w_ref, cb_ref, g_ref, b_ref, o_ref, ext_ref):
    i = pl.program_id(1)
    ext_ref[0:CONV_HALO, :] = jnp.where(i > 0, ap_ref[...], 0.0)
    ext_ref[CONV_HALO:, :] = ac_ref[...]
    acc = jnp.zeros((TOKEN_TILE, CONV_CH), F32) + cb_ref[...]
    base = CONV_HALO - (CONV_WIDTH - 1)
    for j in range(CONV_WIDTH):
        acc = acc + ext_ref[base + j:base + j + TOKEN_TILE, :] * w_ref[j:j + 1, :]
    y = _layer_norm(acc, g_ref[...], b_ref[...])
    o_ref[...] = (y * jax.nn.sigmoid(y)).astype(BF16)


def _conv_branch(a, conv_w, conv_b, ln_g, ln_b, batch, seq):
    T = a.shape[0]
    nt = seq // TOKEN_TILE
    halo_per_tile = TOKEN_TILE // CONV_HALO
    cur = pl.BlockSpec((TOKEN_TILE, CONV_CH), lambda b, i: (b * nt + i, 0))
    prev = pl.BlockSpec((CONV_HALO, CONV_CH),
                        lambda b, i: (jnp.maximum((b * nt + i) * halo_per_tile - 1, 0), 0))
    full = lambda s: pl.BlockSpec(s, lambda b, i: (0, 0))
    return pl.pallas_call(
        _conv_kernel,
        grid=(batch, nt),
        in_specs=[prev, cur, full((CONV_WIDTH, CONV_CH)), full((1, CONV_CH)),
                  full((1, CONV_CH)), full((1, CONV_CH))],
        out_specs=cur,
        out_shape=jax.ShapeDtypeStruct((T, CONV_CH), BF16),
        scratch_shapes=[pltpu.VMEM((CONV_HALO + TOKEN_TILE, CONV_CH), F32)],
        compiler_params=_cparams(("parallel", "parallel")),
        name="conformer_conv",
    )(a, a, conv_w, conv_b, ln_g, ln_b)


def _store_token_tiles(ref, x, lead=()):
    rows = x.shape[0]
    for q in range(SUBLANES):
        ref[lead + (pl.ds(q, rows, stride=SUBLANES), slice(None))] = x[:, q * LANES:(q + 1) * LANES]


def _load_token_tiles(ref, rows, lead=()):
    return jnp.concatenate(
        [ref[lead + (pl.ds(q, rows, stride=SUBLANES), slice(None))] for q in range(SUBLANES)], axis=1)


def _split_bf16(x):
    hi = x.astype(BF16)
    lo = (x - hi.astype(F32)).astype(BF16)
    return hi, lo


def _merge_kernel(o_ref, yc_ref, g_ref, x_ref, wa_ref, wc_ref, bc_ref, wo_ref, lng_ref, lnb_ref,
                  wrh_ref, wrl_ref, br_ref, x1_ref, lg_ref):
    ya = jnp.dot(o_ref[...], wa_ref[...], preferred_element_type=F32)
    yc = jnp.dot(yc_ref[...], wc_ref[...], preferred_element_type=F32) + bc_ref[...]
    merged = g_ref[:, :D_MODEL] * ya + g_ref[:, D_MODEL:] * yc
    z = DEEPNORM_ALPHA * x_ref[...] + jnp.dot(merged.astype(BF16), wo_ref[...],
                                              preferred_element_type=F32)
    x1 = _layer_norm(z, lng_ref[...], lnb_ref[...])
    _store_token_tiles(x1_ref, x1)
    hi, lo = _split_bf16(x1)
    wh = wrh_ref[...]
    lg = (jnp.dot(hi, wh, preferred_element_type=F32)
          + (jnp.dot(hi, wrl_ref[...], preferred_element_type=F32)
             + jnp.dot(lo, wh, preferred_element_type=F32)))
    lg_ref[...] = lg + br_ref[...]


def _merge(o, yc, g, x2, wa, wc, bc, wo, lng, lnb, wrh, wrl, br):
    T = x2.shape[0]
    row = lambda w: pl.BlockSpec((TOKEN_TILE, w), lambda i: (i, 0))
    full = lambda s: pl.BlockSpec(s, lambda i: (0, 0))
    return pl.pallas_call(
        _merge_kernel,
        grid=(T // TOKEN_TILE,),
        in_specs=[row(ATTN_WIDTH), row(CONV_CH), row(2 * D_MODEL), row(D_MODEL),
                  full((ATTN_WIDTH, D_MODEL)), full((CONV_CH, D_MODEL)), full((1, D_MODEL)),
                  full((D_MODEL, D_MODEL)), full((1, D_MODEL)), full((1, D_MODEL)),
                  full((D_MODEL, LANES)), full((D_MODEL, LANES)), full((1, LANES))],
        out_specs=[pl.BlockSpec((TOKEN_TILE * SUBLANES, LANES), lambda i: (i, 0)), row(LANES)],
        out_shape=[jax.ShapeDtypeStruct((T * SUBLANES, LANES), F32), jax.ShapeDtypeStruct((T, LANES), F32)],
        compiler_params=_cparams(("parallel",)),
        name="merge_ln1_router",
    )(o, yc, g, x2, wa, wc, bc, wo, lng, lnb, wrh, wrl, br)


def _route_kernel(lg_ref, idx_ref, gate_ref, rank_ref, cnt_ref, carry_ref):
    @pl.when(pl.program_id(0) == 0)
    def _():
        carry_ref[...] = jnp.zeros_like(carry_ref)

    work = lg_ref[...]
    shape = work.shape
    lane_i = lax.broadcasted_iota(jnp.int32, shape, 1)
    lane = lane_i.astype(F32)
    vals, hots, firsts = [], [], []
    for _ in range(TOP_K):
        m = jnp.max(work, axis=-1, keepdims=True)
        first = jnp.min(jnp.where(work == m, lane, float(LANES)), axis=-1, keepdims=True)
        hot = lane == first
        vals.append(m)
        hots.append(hot)
        firsts.append(first)
        work = jnp.where(hot, -jnp.inf, work)
    exps = [jnp.exp(v - vals[0]) for v in vals]
    denom = exps[0] + exps[1] + exps[2] + exps[3]
    sel = hots[0] | hots[1] | hots[2] | hots[3]
    sel_f = jnp.where(sel, 1.0, 0.0)
    row = lax.broadcasted_iota(jnp.int32, (shape[0], shape[0]), 0)
    col = lax.broadcasted_iota(jnp.int32, (shape[0], shape[0]), 1)
    earlier = jnp.where(row > col, 1.0, 0.0).astype(BF16)
    prefix = jnp.dot(earlier, sel_f.astype(BF16), preferred_element_type=F32) + carry_ref[...]
    idx = jnp.zeros(shape, F32)
    gate = jnp.zeros(shape, F32)
    rank = jnp.zeros(shape, F32)
    for k in range(TOP_K):
        here = lane_i == k
        rank_k = jnp.sum(jnp.where(hots[k], prefix, 0.0), axis=-1, keepdims=True)
        idx = jnp.where(here, firsts[k], idx)
        gate = jnp.where(here, exps[k] / denom, gate)
        rank = jnp.where(here, rank_k, rank)
    idx_ref[...] = idx.astype(jnp.int32)
    gate_ref[...] = gate
    rank_ref[...] = rank.astype(jnp.int32)
    carry_ref[...] = carry_ref[...] + jnp.sum(sel_f, axis=0, keepdims=True)
    cnt_ref[...] = carry_ref[...].astype(jnp.int32)


def _route(logits):
    T = logits.shape[0]
    row = pl.BlockSpec((TOKEN_TILE, LANES), lambda i: (i, 0))
    return pl.pallas_call(
        _route_kernel,
        grid=(T // TOKEN_TILE,),
        in_specs=[row],
        out_specs=[row, row, row, pl.BlockSpec((1, LANES), lambda i: (0, 0))],
        out_shape=[jax.ShapeDtypeStruct((T, LANES), jnp.int32),
                   jax.ShapeDtypeStruct((T, LANES), F32),
                   jax.ShapeDtypeStruct((T, LANES), jnp.int32),
                   jax.ShapeDtypeStruct((1, LANES), jnp.int32)],
        scratch_shapes=[pltpu.VMEM((1, LANES), F32)],
        compiler_params=_cparams(("arbitrary",)),
        name="route_topk_rank",
    )(logits)


def _expert_kernel(te_ref, first_ref, nxt_ref, nu_ref,
                   tok0_ref, tok1_ref, tok2_ref, dstp_ref, dstc_ref, x1_hbm, wup_hbm, wdn_hbm, bup_ref, bdn_ref,
                   out_hbm,
                   wup_f32, wdn_f32, wdn_perm, wup_bf, wdn_bf, xbuf, ybuf, wsem, gsem, ssem):
    s = pl.program_id(0)
    n_used = nu_ref[0]
    slot = lax.rem(s, RING)
    slot_p1 = lax.rem(s + 1, RING)
    slot_p2 = lax.rem(s + 2, RING)
    tile_rows = EXPERT_TILE * SUBLANES
    n_chunks = D_EXPERT // LANES

    def weight_copies(e):
        return (pltpu.make_async_copy(wup_hbm.at[e], wup_f32, wsem.at[0]),
                pltpu.make_async_copy(wdn_hbm.at[e], wdn_f32, wsem.at[1]))

    def token_copy(src, dst, sem, src_tok, dst_tok):
        rows = lambda t: pl.ds(pl.multiple_of(t * SUBLANES, SUBLANES), SUBLANES)
        return pltpu.make_async_copy(src.at[rows(src_tok), :], dst.at[rows(dst_tok), :], sem)

    def wait_tokens(sem):
        def body(g, c):
            for _ in range(DMA_UNROLL):
                token_copy(x1_hbm, xbuf.at[0], sem, 0, 0).wait()
            return c
        lax.fori_loop(0, EXPERT_TILE // DMA_UNROLL, body, 0)

    @pl.when(s == 0)
    def _():
        for cp in weight_copies(te_ref[0]):
            cp.start()

        def first_gathers(g, c):
            for j in range(DMA_UNROLL):
                r = g * DMA_UNROLL + j
                token_copy(x1_hbm, xbuf.at[0], gsem.at[0], tok0_ref[0, 0, r], r).start(priority=0)
                token_copy(x1_hbm, xbuf.at[1], gsem.at[1], tok1_ref[0, 0, r], r).start(priority=1)
            return c

        lax.fori_loop(0, EXPERT_TILE // DMA_UNROLL, first_gathers, 0)
        ybuf[...] = jnp.zeros_like(ybuf)
        spare0 = out_hbm.shape[0] - 2 * tile_rows
        fills = [pltpu.make_async_copy(ybuf.at[b], out_hbm.at[pl.ds(spare0 + b * tile_rows, tile_rows), :],
                                       ssem.at[b]) for b in range(2)]
        for cp in fills:
            cp.start()
        for cp in fills:
            cp.wait()

    @pl.when(s < n_used)
    def _():
        @pl.when(first_ref[s] == 1)
        def _():
            for cp in weight_copies(te_ref[s]):
                cp.wait()

            def cast_up(i, c):
                r = pl.multiple_of(i * CAST_ROWS, CAST_ROWS)
                wup_bf[pl.ds(r, CAST_ROWS), :] = wup_f32[pl.ds(r, CAST_ROWS), :].astype(BF16)
                return c

            lax.fori_loop(0, D_MODEL // CAST_ROWS, cast_up, 0)

            half = LANES // 2

            def permute_down(c, carry):
                r = pl.multiple_of(c * LANES, LANES)
                for j in range(n_chunks):
                    cols = slice(j * LANES, (j + 1) * LANES)
                    wdn_perm[j, pl.ds(r, half, stride=2), :] = wdn_f32[pl.ds(r, half), cols]
                    wdn_perm[j, pl.ds(r + 1, half, stride=2), :] = wdn_f32[pl.ds(r + half, half), cols]
                return carry

            lax.fori_loop(0, n_chunks, permute_down, 0)

            def cast_down(i, c):
                r = pl.multiple_of(i * CAST_ROWS, CAST_ROWS)
                for j in range(n_chunks):
                    wdn_bf[j, pl.ds(r, CAST_ROWS), :] = wdn_perm[j, pl.ds(r, CAST_ROWS), :].astype(BF16)
                return c

            lax.fori_loop(0, D_EXPERT // CAST_ROWS, cast_down, 0)

            @pl.when(nxt_ref[s] >= 0)
            def _():
                for cp in weight_copies(nxt_ref[s]):
                    cp.start()

        wait_tokens(gsem.at[slot])

        @pl.when(s >= 2)
        def _():
            wait_tokens(ssem.at[slot])

        x = _load_token_tiles(xbuf, EXPERT_TILE, (slot,)).astype(BF16)
        for r in range(EXPERT_TILE):
            token_copy(x1_hbm, xbuf.at[slot_p2], gsem.at[slot_p2], tok2_ref[0, 0, r], r).start(priority=0)
            token_copy(ybuf.at[slot_p2], out_hbm, ssem.at[slot_p2], r, dstp_ref[0, 0, r]).start(priority=1)
        up = jnp.dot(x, wup_bf[...], preferred_element_type=F32) + bup_ref[0]
        even = (lax.broadcasted_iota(jnp.int32, (EXPERT_TILE, LANES), 1) & 1) == 0
        acts = []
        for c in range(D_EXPERT // LANES):
            a = up[:, 2 * c * LANES:(2 * c + 1) * LANES]
            b = up[:, (2 * c + 1) * LANES:(2 * c + 2) * LANES]
            glu = jnp.where(even, a, pltpu.roll(b, 1, axis=1))
            lin = jnp.where(even, pltpu.roll(a, LANES - 1, axis=1), b)
            glu = jnp.minimum(glu, SWIGLU_LIMIT)
            lin = jnp.clip(lin, -SWIGLU_LIMIT, SWIGLU_LIMIT)
            acts.append((glu * jax.nn.sigmoid(SWIGLU_ALPHA * glu) * (lin + 1.0)).astype(BF16))
        act = jnp.concatenate(acts, axis=1)
        wdn = jnp.concatenate([wdn_bf[j] for j in range(n_chunks)], axis=1)
        y = jnp.dot(act, wdn, preferred_element_type=F32) + bdn_ref[0]
        _store_token_tiles(ybuf, y, (slot,))

        @pl.when(s == n_used - 1)
        def _():
            def last_scatter(g, c):
                for j in range(DMA_UNROLL):
                    r = g * DMA_UNROLL + j
                    token_copy(ybuf.at[slot], out_hbm, ssem.at[slot], r, dstc_ref[0, 0, r]).start(priority=j % 2)
                return c

            lax.fori_loop(0, EXPERT_TILE // DMA_UNROLL, last_scatter, 0)

            @pl.when(s >= 1)
            def _():
                wait_tokens(ssem.at[slot_p1])
            wait_tokens(ssem.at[slot_p2])
            wait_tokens(ssem.at[slot])
            wait_tokens(gsem.at[slot_p1])
            wait_tokens(gsem.at[slot_p2])


def _experts(tile_expert, tile_first, tile_next, n_used, slot_tok3, slot_dst3, x1t, w_up, w_down,
             bup, bdn, n_out_tokens):
    n_tiles = slot_tok3.shape[0]
    spare = (n_out_tokens - EXPERT_TILE + jnp.arange(EXPERT_TILE, dtype=jnp.int32)).reshape(1, 1, EXPERT_TILE)
    slot_dst_prev3 = jnp.concatenate([spare, slot_dst3[:-1]], axis=0)
    tile_rows = EXPERT_TILE * SUBLANES
    smem_tile = lambda f: pl.BlockSpec((1, 1, EXPERT_TILE), f, memory_space=pltpu.SMEM)
    by_expert = lambda w: pl.BlockSpec((1, 1, w), lambda s, te, *_: (te[s], 0, 0))
    grid_spec = pltpu.PrefetchScalarGridSpec(
        num_scalar_prefetch=4,
        grid=(n_tiles,),
        in_specs=[smem_tile(lambda s, *_: (s, 0, 0)),
                  smem_tile(lambda s, *_: (jnp.minimum(s + 1, n_tiles - 1), 0, 0)),
                  smem_tile(lambda s, *_: (jnp.minimum(s + 2, n_tiles - 1), 0, 0)),
                  smem_tile(lambda s, *_: (s, 0, 0)),
                  smem_tile(lambda s, *_: (s, 0, 0)),
                  pl.BlockSpec(memory_space=pl.ANY),
                  pl.BlockSpec(memory_space=pl.ANY),
                  pl.BlockSpec(memory_space=pl.ANY),
                  by_expert(2 * D_EXPERT), by_expert(D_MODEL)],
        out_specs=pl.BlockSpec(memory_space=pl.ANY),
        scratch_shapes=[pltpu.VMEM((D_MODEL, 2 * D_EXPERT), F32),
                        pltpu.VMEM((D_EXPERT, D_MODEL), F32),
                        pltpu.VMEM((D_MODEL // LANES, D_EXPERT, LANES), F32),
                        pltpu.VMEM((D_MODEL, 2 * D_EXPERT), BF16),
                        pltpu.VMEM((D_MODEL // LANES, D_EXPERT, LANES), BF16),
                        pltpu.VMEM((RING, tile_rows, LANES), F32),
                        pltpu.VMEM((RING, tile_rows, LANES), F32),
                        pltpu.SemaphoreType.DMA((2,)),
                        pltpu.SemaphoreType.DMA((RING,)),
                        pltpu.SemaphoreType.DMA((RING,))],
    )
    return pl.pallas_call(
        _expert_kernel,
        grid_spec=grid_spec,
        out_shape=jax.ShapeDtypeStruct((n_out_tokens * SUBLANES, LANES), F32),
        compiler_params=_cparams(("arbitrary",)),
        name="expert_ffn",
    )(tile_expert, tile_first, tile_next, n_used, slot_tok3, slot_tok3, slot_tok3, slot_dst_prev3, slot_dst3,
      x1t, w_up, w_down, bup, bdn)


def _combine_kernel(gate_ref, x1_ref, y0_ref, y1_ref, y2_ref, y3_ref, lng_ref, lnb_ref, out_ref):
    gate = gate_ref[...]
    zs = []
    for q in range(SUBLANES):
        rows = pl.ds(q, COMBINE_TILE, stride=SUBLANES)
        y = gate[:, 0:1] * y0_ref[rows, :]
        for k, y_ref in ((1, y1_ref), (2, y2_ref), (3, y3_ref)):
            y = y + gate[:, k:k + 1] * y_ref[rows, :]
        zs.append(DEEPNORM_ALPHA * x1_ref[rows, :] + y)
    out_ref[...] = _layer_norm(jnp.concatenate(zs, axis=1), lng_ref[...], lnb_ref[...])


def _combine(gate, x1t, out4t, lng, lnb):
    T = gate.shape[0]
    nt = T // COMBINE_TILE
    row = lambda w: pl.BlockSpec((COMBINE_TILE, w), lambda i: (i, 0))
    tiles = lambda k: pl.BlockSpec((COMBINE_TILE * SUBLANES, LANES), lambda i: (k * nt + i, 0))
    full = lambda s: pl.BlockSpec(s, lambda i: (0, 0))
    return pl.pallas_call(
        _combine_kernel,
        grid=(nt,),
        in_specs=[row(LANES), tiles(0), tiles(0), tiles(1), tiles(2), tiles(3),
                  full((1, D_MODEL)), full((1, D_MODEL))],
        out_specs=row(D_MODEL),
        out_shape=jax.ShapeDtypeStruct((T, D_MODEL), F32),
        compiler_params=_cparams(("parallel",)),
        name="combine_ln2",
    )(gate, x1t, out4t, out4t, out4t, out4t, lng, lnb)


def _lookup(table, idx):
    hit = idx[..., None] == jnp.arange(table.shape[0], dtype=idx.dtype)
    return jnp.sum(jnp.where(hit, table, 0), axis=-1)


def _layer(x2, batch, seq, w_in, b_in, sinks, w_attn_br, conv_w, conv_b, conv_ln_g, conv_ln_b,
           w_conv_br, b_conv_br, w_o, ln1_g, ln1_b, w_router, b_router, w_up, b_up, w_down, b_down,
           ln2_g, ln2_b):
    T = x2.shape[0]
    r2 = lambda v: v.reshape(1, -1)
    q, k, v, a, g = _inproj(x2, w_in.astype(BF16), r2(b_in))
    o = _attention(q, k, v, sinks, batch, seq)
    yc = _conv_branch(a, conv_w, r2(conv_b), r2(conv_ln_g), r2(conv_ln_b), batch, seq)

    wr = jnp.pad(w_router, ((0, 0), (0, LANES - N_EXPERTS)))
    wr_hi, wr_lo = _split_bf16(wr)
    br = jnp.pad(b_router, (0, LANES - N_EXPERTS), constant_values=NEG_BIG)
    x1t, logits = _merge(o, yc, g, x2, w_attn_br.astype(BF16), w_conv_br.astype(BF16), r2(b_conv_br),
                        w_o.astype(BF16), r2(ln1_g), r2(ln1_b), wr_hi, wr_lo, r2(br))

    i32 = jnp.int32
    idx, gate, rank, cnt = _route(logits)
    n_assign = T * TOP_K
    n_pad = N_EXPERTS * EXPERT_TILE
    n_slots = n_assign + n_pad
    n_tiles = n_slots // EXPERT_TILE
    counts = cnt[0, :N_EXPERTS]
    padded = (counts + EXPERT_TILE - 1) // EXPERT_TILE * EXPERT_TILE
    pad_end = jnp.cumsum(padded)
    pad_start = pad_end - padded
    pos = _lookup(pad_start, idx[:, :TOP_K]) + rank[:, :TOP_K]
    pad_cnt = padded - counts
    pad_cum = jnp.cumsum(pad_cnt)
    d = jnp.arange(n_pad, dtype=i32)
    e_d = jnp.minimum(jnp.sum(pad_cum[None, :] <= d[:, None], axis=1), N_EXPERTS - 1).astype(i32)
    key_in = _lookup(pad_start + counts - (pad_cum - pad_cnt), e_d) + d
    key_tail = pad_end[-1] + (d - pad_cum[-1])
    pad_keys = jnp.where(d < pad_cum[-1], key_in, key_tail)
    keys = jnp.concatenate([pos.reshape(-1), pad_keys]).astype(i32)
    vals = jnp.concatenate([jnp.arange(n_assign, dtype=i32), jnp.full((n_pad,), -1, i32)])
    _, slot_a = lax.sort_key_val(keys, vals)
    slot_i = jnp.arange(n_slots, dtype=i32)
    real = slot_a >= 0
    slot_tok = jnp.where(real, slot_a // TOP_K, 0)
    slot_dst = jnp.where(real, (slot_a % TOP_K) * T + slot_a // TOP_K, n_assign + slot_i % (2 * EXPERT_TILE))

    tile_start = jnp.arange(n_tiles, dtype=i32) * EXPERT_TILE
    tile_expert = jnp.minimum(jnp.sum(pad_end[None, :] <= tile_start[:, None], axis=1), N_EXPERTS - 1).astype(i32)
    n_used = (pad_end[-1] // EXPERT_TILE).astype(i32)
    tile_first = (tile_start == _lookup(pad_start, tile_expert)).astype(i32)
    group_end = _lookup(pad_end, tile_expert) // EXPERT_TILE
    tile_next = jnp.where(group_end < n_used,
                          jnp.take(tile_expert, jnp.minimum(group_end, n_tiles - 1)), -1).astype(i32)

    out4t = _experts(tile_expert, tile_first, tile_next, n_used.reshape(1),
                     slot_tok.reshape(n_tiles, 1, EXPERT_TILE), slot_dst.reshape(n_tiles, 1, EXPERT_TILE),
                     x1t, w_up, w_down, b_up[:, None, :], b_down[:, None, :], n_assign + 2 * EXPERT_TILE)
    return _combine(gate, x1t, out4t, r2(ln2_g), r2(ln2_b))


def kernel(x, w_in, b_in, attn_sinks, w_attn_br, conv_w, conv_b, conv_ln_g, conv_ln_b, w_conv_br,
           b_conv_br, w_o, ln1_g, ln1_b, w_router, b_router, w_up, b_up, w_down, b_down, ln2_g, ln2_b):
    B, S, D = x.shape
    x2 = x.reshape(B * S, D)
    for l in range(DEPTH):
        x2 = _layer(x2, B, S, w_in[l], b_in[l], attn_sinks[l], w_attn_br[l], conv_w[l], conv_b[l],
                    conv_ln_g[l], conv_ln_b[l], w_conv_br[l], b_conv_br[l], w_o[l], ln1_g[l], ln1_b[l],
                    w_router[l], b_router[l], w_up[l], b_up[l], w_down[l], b_down[l], ln2_g[l], ln2_b[l])
    return x2.reshape(B, S, D)
```

```python
import functools

import jax
import jax.numpy as jnp
from jax import lax
from jax.experimental import pallas as pl
from jax.experimental.pallas import tpu as pltpu

F32 = jnp.float32
BF16 = jnp.bfloat16

D_MODEL = 1024
N_HEADS = 8
N_KV_HEADS = 2
HEAD_DIM = 64
ATTN_WIDTH = N_HEADS * HEAD_DIM
KV_WIDTH = N_KV_HEADS * HEAD_DIM
WINDOW = 128
ATTN_BLOCK = 128
CONV_CH = 512
CONV_WIDTH = 31
Q_END = ATTN_WIDTH
K_END = Q_END + KV_WIDTH
V_END = K_END + KV_WIDTH
CONV_END = V_END + 2 * CONV_CH
IN_WIDTH = CONV_END + 2 * D_MODEL
N_EXPERTS = 32
TOP_K = 4
D_EXPERT = D_MODEL
SWIGLU_ALPHA = 1.702
SWIGLU_LIMIT = 7.0
LN_EPS = 1e-5
DEPTH = 1
DEEPNORM_ALPHA = (2 * DEPTH) ** 0.25

LANES = 128
SUBLANES = 8
NEG_BIG = -1e30
TOKEN_TILE = 512
CONV_HALO = 32
CONV_ROWS = 32
EXPERT_TILE = 256
COMBINE_TILE = 256
CAST_ROWS = 64
DMA_UNROLL = 8
RING = 3
VMEM_LIMIT = 56 * 1024 * 1024


def _layer_norm(z, g, b):
    mu = jnp.mean(z, axis=-1, keepdims=True)
    zc = z - mu
    var = jnp.mean(zc * zc, axis=-1, keepdims=True)
    return zc * lax.rsqrt(var + LN_EPS) * g + b


def _cparams(sem):
    return pltpu.CompilerParams(dimension_semantics=sem, vmem_limit_bytes=VMEM_LIMIT)


def _inproj_kernel(x_ref, w_ref, b_ref, q_ref, k_ref, v_ref, a_ref, g_ref):
    x = x_ref[...].astype(BF16)

    def proj(lo, hi):
        return jnp.dot(x, w_ref[:, lo:hi], preferred_element_type=F32) + b_ref[:, lo:hi]

    qkv = proj(0, V_END)
    q_ref[...] = qkv[:, :Q_END].astype(BF16)
    k_ref[...] = qkv[:, Q_END:K_END].astype(BF16)
    v_ref[...] = qkv[:, K_END:V_END].astype(BF16)
    c = proj(V_END, CONV_END)
    a_ref[...] = c[:, :CONV_CH] * jax.nn.sigmoid(c[:, CONV_CH:])
    g_ref[...] = jax.nn.sigmoid(proj(CONV_END, IN_WIDTH)).astype(BF16)


def _inproj(x2, w_in, b_in):
    T = x2.shape[0]
    row = lambda w: pl.BlockSpec((TOKEN_TILE, w), lambda i: (i, 0))
    full = lambda s: pl.BlockSpec(s, lambda i: (0, 0))
    return pl.pallas_call(
        _inproj_kernel,
        grid=(T // TOKEN_TILE,),
        in_specs=[row(D_MODEL), full((D_MODEL, IN_WIDTH)), full((1, IN_WIDTH))],
        out_specs=[row(ATTN_WIDTH), row(KV_WIDTH), row(KV_WIDTH), row(CONV_CH), row(2 * D_MODEL)],
        out_shape=[jax.ShapeDtypeStruct((T, ATTN_WIDTH), BF16),
                   jax.ShapeDtypeStruct((T, KV_WIDTH), BF16),
                   jax.ShapeDtypeStruct((T, KV_WIDTH), BF16),
                   jax.ShapeDtypeStruct((T, CONV_CH), F32),
                   jax.ShapeDtypeStruct((T, 2 * D_MODEL), BF16)],
        compiler_params=_cparams(("parallel",)),
        name="inproj",
    )(x2, w_in, b_in)


def _attn_kernel(sink_ref, q_ref, kc_ref, kp_ref, vc_ref, vp_ref, o_ref):
    i = pl.program_id(1)
    q = q_ref[...]
    kk = jnp.concatenate([kp_ref[...], kc_ref[...]], axis=0)
    vv = jnp.concatenate([vp_ref[...], vc_ref[...]], axis=0)
    qi = lax.broadcasted_iota(jnp.int32, (ATTN_BLOCK, 2 * ATTN_BLOCK), 0)
    kj = lax.broadcasted_iota(jnp.int32, (ATTN_BLOCK, 2 * ATTN_BLOCK), 1)
    delta = qi + ATTN_BLOCK - kj
    mask = (delta >= 0) & (delta < WINDOW) & ((kj >= ATTN_BLOCK) | (i > 0))
    group = N_HEADS // N_KV_HEADS
    outs = []
    for h in range(N_HEADS):
        g = h // group
        qh = q[:, h * HEAD_DIM:(h + 1) * HEAD_DIM]
        kh = kk[:, g * HEAD_DIM:(g + 1) * HEAD_DIM]
        vh = vv[:, g * HEAD_DIM:(g + 1) * HEAD_DIM]
        s = lax.dot_general(qh, kh, (((1,), (1,)), ((), ())),
                            preferred_element_type=F32) * (HEAD_DIM ** -0.5)
        s = jnp.where(mask, s, NEG_BIG)
        sink = sink_ref[h]
        m = jnp.maximum(jnp.max(s, axis=-1, keepdims=True), sink)
        p = jnp.exp(s - m)
        denom = jnp.sum(p, axis=-1, keepdims=True) + jnp.exp(sink - m)
        p = p / denom
        outs.append(jnp.dot(p.astype(BF16), vh, preferred_element_type=F32))
    o_ref[...] = jnp.concatenate(outs, axis=1).astype(BF16)


def _attention(q, k, v, sinks, batch, seq):
    T = q.shape[0]
    nb = seq // ATTN_BLOCK
    cur = lambda w: pl.BlockSpec((ATTN_BLOCK, w), lambda b, i: (b * nb + i, 0))
    prev = lambda w: pl.BlockSpec((ATTN_BLOCK, w), lambda b, i: (b * nb + jnp.maximum(i - 1, 0), 0))
    return pl.pallas_call(
        _attn_kernel,
        grid=(batch, nb),
        in_specs=[pl.BlockSpec(memory_space=pltpu.SMEM),
                  cur(ATTN_WIDTH), cur(KV_WIDTH), prev(KV_WIDTH), cur(KV_WIDTH), prev(KV_WIDTH)],
        out_specs=cur(ATTN_WIDTH),
        out_shape=jax.ShapeDtypeStruct((T, ATTN_WIDTH), BF16),
        compiler_params=_cparams(("parallel", "parallel")),
        name="swa_attention",
    )(sinks, q, k, k, v, v)


def _conv_kernel(ap_ref, ac_ref, w_ref, cb_ref, g_ref, b_ref, o_ref, ext_ref, sh_ref):
    i = pl.program_id(1)
    ext_ref[0:CONV_HALO, :] = jnp.where(i > 0, ap_ref[...], 0.0)
    ext_ref[CONV_HALO:, :] = ac_ref[...]
    span = sh_ref.shape[1]
    for r in range(1, SUBLANES):
        sh_ref[r - 1] = ext_ref[r:r + span, :]
    base = CONV_HALO - (CONV_WIDTH - 1)

    def rows_chunk(c, carry):
        row0 = pl.multiple_of(c * CONV_ROWS, CONV_ROWS)
        acc = jnp.zeros((CONV_ROWS, CONV_CH), F32) + cb_ref[...]
        for j in range(CONV_WIDTH):
            r = (base + j) % SUBLANES
            rows = pl.ds(row0 + (base + j - r), CONV_ROWS)
            tap = ext_ref[rows, :] if r == 0 else sh_ref[r - 1, rows, :]
            acc = acc + tap * w_ref[j:j + 1, :]
        ext_ref[pl.ds(row0, CONV_ROWS), :] = acc
        return carry

    lax.fori_loop(0, TOKEN_TILE // CONV_ROWS, rows_chunk, 0)
    y = _layer_norm(ext_ref[0:TOKEN_TILE, :], g_ref[...], b_ref[...])
    o_ref[...] = (y * jax.nn.sigmoid(y)).astype(BF16)


def _conv_branch(a, conv_w, conv_b, ln_g, ln_b, batch, seq):
    T = a.shape[0]
    nt = seq // TOKEN_TILE
    halo_per_tile = TOKEN_TILE // CONV_HALO
    cur = pl.BlockSpec((TOKEN_TILE, CONV_CH), lambda b, i: (b * nt + i, 0))
    prev = pl.BlockSpec((CONV_HALO, CONV_CH),
                        lambda b, i: (jnp.maximum((b * nt + i) * halo_per_tile - 1, 0), 0))
    full = lambda s: pl.BlockSpec(s, lambda b, i: (0, 0))
    return pl.pallas_call(
        _conv_kernel,
        grid=(batch, nt),
        in_specs=[prev, cur, full((CONV_WIDTH, CONV_CH)), full((1, CONV_CH)),
                  full((1, CONV_CH)), full((1, CONV_CH))],
        out_specs=cur,
        out_shape=jax.ShapeDtypeStruct((T, CONV_CH), BF16),
        scratch_shapes=[pltpu.VMEM((CONV_HALO + TOKEN_TILE, CONV_CH), F32),
                        pltpu.VMEM((SUBLANES - 1, CONV_HALO + TOKEN_TILE - SUBLANES, CONV_CH), F32)],
        compiler_params=_cparams(("parallel", "parallel")),
        name="conformer_conv",
    )(a, a, conv_w, conv_b, ln_g, ln_b)


def _store_token_tiles(ref, x, lead=()):
    rows = x.shape[0]
    for q in range(SUBLANES):
        ref[lead + (pl.ds(q, rows, stride=SUBLANES), slice(None))] = x[:, q * LANES:(q + 1) * LANES]


def _load_token_tiles(ref, rows, lead=()):
    return jnp.concatenate(
        [ref[lead + (pl.ds(q, rows, stride=SUBLANES), slice(None))] for q in range(SUBLANES)], axis=1)


def _split_bf16(x):
    hi = x.astype(BF16)
    lo = (x - hi.astype(F32)).astype(BF16)
    return hi, lo


def _merge_kernel(o_ref, yc_ref, g_ref, x_ref, wa_ref, wc_ref, bc_ref, wo_ref, lng_ref, lnb_ref,
                  wrh_ref, wrl_ref, br_ref, x1_ref, lg_ref):
    ya = jnp.dot(o_ref[...], wa_ref[...], preferred_element_type=F32)
    yc = jnp.dot(yc_ref[...], wc_ref[...], preferred_element_type=F32) + bc_ref[...]
    merged = g_ref[:, :D_MODEL] * ya + g_ref[:, D_MODEL:] * yc
    z = DEEPNORM_ALPHA * x_ref[...] + jnp.dot(merged.astype(BF16), wo_ref[...],
                                              preferred_element_type=F32)
    x1 = _layer_norm(z, lng_ref[...], lnb_ref[...])
    _store_token_tiles(x1_ref, x1)
    hi, lo = _split_bf16(x1)
    wh = wrh_ref[...]
    lg = (jnp.dot(hi, wh, preferred_element_type=F32)
          + (jnp.dot(hi, wrl_ref[...], preferred_element_type=F32)
             + jnp.dot(lo, wh, preferred_element_type=F32)))
    lg_ref[...] = lg + br_ref[...]


def _merge(o, yc, g, x2, wa, wc, bc, wo, lng, lnb, wrh, wrl, br):
    T = x2.shape[0]
    row = lambda w: pl.BlockSpec((TOKEN_TILE, w), lambda i: (i, 0))
    full = lambda s: pl.BlockSpec(s, lambda i: (0, 0))
    return pl.pallas_call(
        _merge_kernel,
        grid=(T // TOKEN_TILE,),
        in_specs=[row(ATTN_WIDTH), row(CONV_CH), row(2 * D_MODEL), row(D_MODEL),
                  full((ATTN_WIDTH, D_MODEL)), full((CONV_CH, D_MODEL)), full((1, D_MODEL)),
                  full((D_MODEL, D_MODEL)), full((1, D_MODEL)), full((1, D_MODEL)),
                  full((D_MODEL, LANES)), full((D_MODEL, LANES)), full((1, LANES))],
        out_specs=[pl.BlockSpec((TOKEN_TILE * SUBLANES, LANES), lambda i: (i, 0)), row(LANES)],
        out_shape=[jax.ShapeDtypeStruct((T * SUBLANES, LANES), F32), jax.ShapeDtypeStruct((T, LANES), F32)],
        compiler_params=_cparams(("parallel",)),
        name="merge_ln1_router",
    )(o, yc, g, x2, wa, wc, bc, wo, lng, lnb, wrh, wrl, br)


def _route_kernel(lg_ref, idx_ref, gate_ref, rank_ref, cnt_ref, carry_ref):
    @pl.when(pl.program_id(0) == 0)
    def _():
        carry_ref[...] = jnp.zeros_like(carry_ref)

    work = lg_ref[...]
    shape = work.shape
    lane_i = lax.broadcasted_iota(jnp.int32, shape, 1)
    lane = lane_i.astype(F32)
    vals, hots, firsts = [], [], []
    for _ in range(TOP_K):
        m = jnp.max(work, axis=-1, keepdims=True)
        first = jnp.min(jnp.where(work == m, lane, float(LANES)), axis=-1, keepdims=True)
        hot = lane == first
        vals.append(m)
        hots.append(hot)
        firsts.append(first)
        work = jnp.where(hot, -jnp.inf, work)
    exps = [jnp.exp(v - vals[0]) for v in vals]
    denom = exps[0] + exps[1] + exps[2] + exps[3]
    sel = hots[0] | hots[1] | hots[2] | hots[3]
    sel_f = jnp.where(sel, 1.0, 0.0)
    row = lax.broadcasted_iota(jnp.int32, (shape[0], shape[0]), 0)
    col = lax.broadcasted_iota(jnp.int32, (shape[0], shape[0]), 1)
    earlier = jnp.where(row > col, 1.0, 0.0).astype(BF16)
    prefix = jnp.dot(earlier, sel_f.astype(BF16), preferred_element_type=F32) + carry_ref[...]
    idx = jnp.zeros(shape, F32)
    gate = jnp.zeros(shape, F32)
    rank = jnp.zeros(shape, F32)
    for k in range(TOP_K):
        here = lane_i == k
        rank_k = jnp.sum(jnp.where(hots[k], prefix, 0.0), axis=-1, keepdims=True)
        idx = jnp.where(here, firsts[k], idx)
        gate = jnp.where(here, exps[k] / denom, gate)
        rank = jnp.where(here, rank_k, rank)
    idx_ref[...] = idx.astype(jnp.int32)
    gate_ref[...] = gate
    rank_ref[...] = rank.astype(jnp.int32)
    carry_ref[...] = carry_ref[...] + jnp.sum(sel_f, axis=0, keepdims=True)
    cnt_ref[...] = carry_ref[...].astype(jnp.int32)


def _route(logits):
    T = logits.shape[0]
    row = pl.BlockSpec((TOKEN_TILE, LANES), lambda i: (i, 0))
    return pl.pallas_call(
        _route_kernel,
        grid=(T // TOKEN_TILE,),
        in_specs=[row],
        out_specs=[row, row, row, pl.BlockSpec((1, LANES), lambda i: (0, 0))],
        out_shape=[jax.ShapeDtypeStruct((T, LANES), jnp.int32),
                   jax.ShapeDtypeStruct((T, LANES), F32),
                   jax.ShapeDtypeStruct((T, LANES), jnp.int32),
                   jax.ShapeDtypeStruct((1, LANES), jnp.int32)],
        scratch_shapes=[pltpu.VMEM((1, LANES), F32)],
        compiler_params=_cparams(("arbitrary",)),
        name="route_topk_rank",
    )(logits)


def _expert_kernel(te_ref, first_ref, nxt_ref, nu_ref,
                   tok0_ref, tok1_ref, tok2_ref, dstp_ref, dstc_ref, x1_hbm, wup_hbm, wdn_hbm, bup_ref, bdn_ref,
                   out_hbm,
                   wup_f32, wdn_f32, wdn_perm, wup_bf, wdn_bf, xbuf, ybuf, wsem, gsem, ssem):
    s = pl.program_id(0)
    n_used = nu_ref[0]
    slot = lax.rem(s, RING)
    slot_p1 = lax.rem(s + 1, RING)
    slot_p2 = lax.rem(s + 2, RING)
    tile_rows = EXPERT_TILE * SUBLANES
    n_chunks = D_EXPERT // LANES

    def weight_copies(e):
        return (pltpu.make_async_copy(wup_hbm.at[e], wup_f32, wsem.at[0]),
                pltpu.make_async_copy(wdn_hbm.at[e], wdn_f32, wsem.at[1]))

    def token_copy(src, dst, sem, src_tok, dst_tok):
        rows = lambda t: pl.ds(pl.multiple_of(t * SUBLANES, SUBLANES), SUBLANES)
        return pltpu.make_async_copy(src.at[rows(src_tok), :], dst.at[rows(dst_tok), :], sem)

    def wait_tokens(sem):
        def body(g, c):
            for _ in range(DMA_UNROLL):
                token_copy(x1_hbm, xbuf.at[0], sem, 0, 0).wait()
            return c
        lax.fori_loop(0, EXPERT_TILE // DMA_UNROLL, body, 0)

    @pl.when(s == 0)
    def _():
        for cp in weight_copies(te_ref[0]):
            cp.start()

        def first_gathers(g, c):
            for j in range(DMA_UNROLL):
                r = g * DMA_UNROLL + j
                token_copy(x1_hbm, xbuf.at[0], gsem.at[0], tok0_ref[0, 0, r], r).start(priority=0)
                token_copy(x1_hbm, xbuf.at[1], gsem.at[1], tok1_ref[0, 0, r], r).start(priority=1)
            return c

        lax.fori_loop(0, EXPERT_TILE // DMA_UNROLL, first_gathers, 0)
        ybuf[...] = jnp.zeros_like(ybuf)
        spare0 = out_hbm.shape[0] - 2 * tile_rows
        fills = [pltpu.make_async_copy(ybuf.at[b], out_hbm.at[pl.ds(spare0 + b * tile_rows, tile_rows), :],
                                       ssem.at[b]) for b in range(2)]
        for cp in fills:
            cp.start()
        for cp in fills:
            cp.wait()

    @pl.when(s < n_used)
    def _():
        @pl.when(first_ref[s] == 1)
        def _():
            for cp in weight_copies(te_ref[s]):
                cp.wait()

            def cast_up(i, c):
                r = pl.multiple_of(i * CAST_ROWS, CAST_ROWS)
                wup_bf[pl.ds(r, CAST_ROWS), :] = wup_f32[pl.ds(r, CAST_ROWS), :].astype(BF16)
                return c

            lax.fori_loop(0, D_MODEL // CAST_ROWS, cast_up, 0)

            half = LANES // 2

            def permute_down(c, carry):
                r = pl.multiple_of(c * LANES, LANES)
                for j in range(n_chunks):
                    cols = slice(j * LANES, (j + 1) * LANES)
                    wdn_perm[j, pl.ds(r, half, stride=2), :] = wdn_f32[pl.ds(r, half), cols]
                    wdn_perm[j, pl.ds(r + 1, half, stride=2), :] = wdn_f32[pl.ds(r + half, half), cols]
                return carry

            lax.fori_loop(0, n_chunks, permute_down, 0)

            def cast_down(i, c):
                r = pl.multiple_of(i * CAST_ROWS, CAST_ROWS)
                for j in range(n_chunks):
                    wdn_bf[j, pl.ds(r, CAST_ROWS), :] = wdn_perm[j, pl.ds(r, CAST_ROWS), :].astype(BF16)
                return c

            lax.fori_loop(0, D_EXPERT // CAST_ROWS, cast_down, 0)

            @pl.when(nxt_ref[s] >= 0)
            def _():
                for cp in weight_copies(nxt_ref[s]):
                    cp.start()

        wait_tokens(gsem.at[slot])

        @pl.when(s >= 2)
        def _():
            wait_tokens(ssem.at[slot])

        x = _load_token_tiles(xbuf, EXPERT_TILE, (slot,)).astype(BF16)
        for r in range(EXPERT_TILE):
            token_copy(x1_hbm, xbuf.at[slot_p2], gsem.at[slot_p2], tok2_ref[0, 0, r], r).start(priority=0)
            token_copy(ybuf.at[slot_p2], out_hbm, ssem.at[slot_p2], r, dstp_ref[0, 0, r]).start(priority=1)
        up = jnp.dot(x, wup_bf[...], preferred_element_type=F32) + bup_ref[0]
        even = (lax.broadcasted_iota(jnp.int32, (EXPERT_TILE, LANES), 1) & 1) == 0
        acts = []
        for c in range(D_EXPERT // LANES):
            a = up[:, 2 * c * LANES:(2 * c + 1) * LANES]
            b = up[:, (2 * c + 1) * LANES:(2 * c + 2) * LANES]
            glu = jnp.where(even, a, pltpu.roll(b, 1, axis=1))
            lin = jnp.where(even, pltpu.roll(a, LANES - 1, axis=1), b)
            glu = jnp.minimum(glu, SWIGLU_LIMIT)
            lin = jnp.clip(lin, -SWIGLU_LIMIT, SWIGLU_LIMIT)
            acts.append((glu * jax.nn.sigmoid(SWIGLU_ALPHA * glu) * (lin + 1.0)).astype(BF16))
        act = jnp.concatenate(acts, axis=1)
        wdn = jnp.concatenate([wdn_bf[j] for j in range(n_chunks)], axis=1)
        y = jnp.dot(act, wdn, preferred_element_type=F32) + bdn_ref[0]
        _store_token_tiles(ybuf, y, (slot,))

        @pl.when(s == n_used - 1)
        def _():
            def last_scatter(g, c):
                for j in range(DMA_UNROLL):
                    r = g * DMA_UNROLL + j
                    token_copy(ybuf.at[slot], out_hbm, ssem.at[slot], r, dstc_ref[0, 0, r]).start(priority=j % 2)
                return c

            lax.fori_loop(0, EXPERT_TILE // DMA_UNROLL, last_scatter, 0)

            @pl.when(s >= 1)
            def _():
                wait_tokens(ssem.at[slot_p1])
            wait_tokens(ssem.at[slot_p2])
            wait_tokens(ssem.at[slot])
            wait_tokens(gsem.at[slot_p1])
            wait_tokens(gsem.at[slot_p2])


def _experts(tile_expert, tile_first, tile_next, n_used, slot_tok3, slot_dst3, x1t, w_up, w_down,
             bup, bdn, n_out_tokens):
    n_tiles = slot_tok3.shape[0]
    spare = (n_out_tokens - EXPERT_TILE + jnp.arange(EXPERT_TILE, dtype=jnp.int32)).reshape(1, 1, EXPERT_TILE)
    slot_dst_prev3 = jnp.concatenate([spare, slot_dst3[:-1]], axis=0)
    tile_rows = EXPERT_TILE * SUBLANES
    smem_tile = lambda f: pl.BlockSpec((1, 1, EXPERT_TILE), f, memory_space=pltpu.SMEM)
    by_expert = lambda w: pl.BlockSpec((1, 1, w), lambda s, te, *_: (te[s], 0, 0))
    grid_spec = pltpu.PrefetchScalarGridSpec(
        num_scalar_prefetch=4,
        grid=(n_tiles,),
        in_specs=[smem_tile(lambda s, *_: (s, 0, 0)),
                  smem_tile(lambda s, *_: (jnp.minimum(s + 1, n_tiles - 1), 0, 0)),
                  smem_tile(lambda s, *_: (jnp.minimum(s + 2, n_tiles - 1), 0, 0)),
                  smem_tile(lambda s, *_: (s, 0, 0)),
                  smem_tile(lambda s, *_: (s, 0, 0)),
                  pl.BlockSpec(memory_space=pl.ANY),
                  pl.BlockSpec(memory_space=pl.ANY),
                  pl.BlockSpec(memory_space=pl.ANY),
                  by_expert(2 * D_EXPERT), by_expert(D_MODEL)],
        out_specs=pl.BlockSpec(memory_space=pl.ANY),
        scratch_shapes=[pltpu.VMEM((D_MODEL, 2 * D_EXPERT), F32),
                        pltpu.VMEM((D_EXPERT, D_MODEL), F32),
                        pltpu.VMEM((D_MODEL // LANES, D_EXPERT, LANES), F32),
                        pltpu.VMEM((D_MODEL, 2 * D_EXPERT), BF16),
                        pltpu.VMEM((D_MODEL // LANES, D_EXPERT, LANES), BF16),
                        pltpu.VMEM((RING, tile_rows, LANES), F32),
                        pltpu.VMEM((RING, tile_rows, LANES), F32),
                        pltpu.SemaphoreType.DMA((2,)),
                        pltpu.SemaphoreType.DMA((RING,)),
                        pltpu.SemaphoreType.DMA((RING,))],
    )
    return pl.pallas_call(
        _expert_kernel,
        grid_spec=grid_spec,
        out_shape=jax.ShapeDtypeStruct((n_out_tokens * SUBLANES, LANES), F32),
        compiler_params=_cparams(("arbitrary",)),
        name="expert_ffn",
    )(tile_expert, tile_first, tile_next, n_used, slot_tok3, slot_tok3, slot_tok3, slot_dst_prev3, slot_dst3,
      x1t, w_up, w_down, bup, bdn)


def _combine_kernel(gate_ref, x1_ref, y0_ref, y1_ref, y2_ref, y3_ref, lng_ref, lnb_ref, out_ref):
    gate = gate_ref[...]
    zs = []
    for q in range(SUBLANES):
        rows = pl.ds(q, COMBINE_TILE, stride=SUBLANES)
        y = gate[:, 0:1] * y0_ref[rows, :]
        for k, y_ref in ((1, y1_ref), (2, y2_ref), (3, y3_ref)):
            y = y + gate[:, k:k + 1] * y_ref[rows, :]
        zs.append(DEEPNORM_ALPHA * x1_ref[rows, :] + y)
    out_ref[...] = _layer_norm(jnp.concatenate(zs, axis=1), lng_ref[...], lnb_ref[...])


def _combine(gate, x1t, out4t, lng, lnb):
    T = gate.shape[0]
    nt = T // COMBINE_TILE
    row = lambda w: pl.BlockSpec((COMBINE_TILE, w), lambda i: (i, 0))
    tiles = lambda k: pl.BlockSpec((COMBINE_TILE * SUBLANES, LANES), lambda i: (k * nt + i, 0))
    full = lambda s: pl.BlockSpec(s, lambda i: (0, 0))
    return pl.pallas_call(
        _combine_kernel,
        grid=(nt,),
        in_specs=[row(LANES), tiles(0), tiles(0), tiles(1), tiles(2), tiles(3),
                  full((1, D_MODEL)), full((1, D_MODEL))],
        out_specs=row(D_MODEL),
        out_shape=jax.ShapeDtypeStruct((T, D_MODEL), F32),
        compiler_params=_cparams(("parallel",)),
        name="combine_ln2",
    )(gate, x1t, out4t, out4t, out4t, out4t, lng, lnb)


def _lookup(table, idx):
    hit = idx[..., None] == jnp.arange(table.shape[0], dtype=idx.dtype)
    return jnp.sum(jnp.where(hit, table, 0), axis=-1)


def _layer(x2, batch, seq, w_in, b_in, sinks, w_attn_br, conv_w, conv_b, conv_ln_g, conv_ln_b,
           w_conv_br, b_conv_br, w_o, ln1_g, ln1_b, w_router, b_router, w_up, b_up, w_down, b_down,
           ln2_g, ln2_b):
    T = x2.shape[0]
    r2 = lambda v: v.reshape(1, -1)
    q, k, v, a, g = _inproj(x2, w_in.astype(BF16), r2(b_in))
    o = _attention(q, k, v, sinks, batch, seq)
    yc = _conv_branch(a, conv_w, r2(conv_b), r2(conv_ln_g), r2(conv_ln_b), batch, seq)

    wr = jnp.pad(w_router, ((0, 0), (0, LANES - N_EXPERTS)))
    wr_hi, wr_lo = _split_bf16(wr)
    br = jnp.pad(b_router, (0, LANES - N_EXPERTS), constant_values=NEG_BIG)
    x1t, logits = _merge(o, yc, g, x2, w_attn_br.astype(BF16), w_conv_br.astype(BF16), r2(b_conv_br),
                        w_o.astype(BF16), r2(ln1_g), r2(ln1_b), wr_hi, wr_lo, r2(br))

    i32 = jnp.int32
    idx, gate, rank, cnt = _route(logits)
    n_assign = T * TOP_K
    n_pad = N_EXPERTS * EXPERT_TILE
    n_slots = n_assign + n_pad
    n_tiles = n_slots // EXPERT_TILE
    counts = cnt[0, :N_EXPERTS]
    padded = (counts + EXPERT_TILE - 1) // EXPERT_TILE * EXPERT_TILE
    pad_end = jnp.cumsum(padded)
    pad_start = pad_end - padded
    pos = _lookup(pad_start, idx[:, :TOP_K]) + rank[:, :TOP_K]
    pad_cnt = padded - counts
    pad_cum = jnp.cumsum(pad_cnt)
    d = jnp.arange(n_pad, dtype=i32)
    e_d = jnp.minimum(jnp.sum(pad_cum[None, :] <= d[:, None], axis=1), N_EXPERTS - 1).astype(i32)
    key_in = _lookup(pad_start + counts - (pad_cum - pad_cnt), e_d) + d
    key_tail = pad_end[-1] + (d - pad_cum[-1])
    pad_keys = jnp.where(d < pad_cum[-1], key_in, key_tail)
    keys = jnp.concatenate([pos.reshape(-1), pad_keys]).astype(i32)
    vals = jnp.concatenate([jnp.arange(n_assign, dtype=i32), jnp.full((n_pad,), -1, i32)])
    _, slot_a = lax.sort_key_val(keys, vals)
    slot_i = jnp.arange(n_slots, dtype=i32)
    real = slot_a >= 0
    slot_tok = jnp.where(real, slot_a // TOP_K, 0)
    slot_dst = jnp.where(real, (slot_a % TOP_K) * T + slot_a // TOP_K, n_assign + slot_i % (2 * EXPERT_TILE))

    tile_start = jnp.arange(n_tiles, dtype=i32) * EXPERT_TILE
    tile_expert = jnp.minimum(jnp.sum(pad_end[None, :] <= tile_start[:, None], axis=1), N_EXPERTS - 1).astype(i32)
    n_used = (pad_end[-1] // EXPERT_TILE).astype(i32)
    tile_first = (tile_start == _lookup(pad_start, tile_expert)).astype(i32)
    group_end = _lookup(pad_end, tile_expert) // EXPERT_TILE
    tile_next = jnp.where(group_end < n_used,
                          jnp.take(tile_expert, jnp.minimum(group_end, n_tiles - 1)), -1).astype(i32)

    out4t = _experts(tile_expert, tile_first, tile_next, n_used.reshape(1),
                     slot_tok.reshape(n_tiles, 1, EXPERT_TILE), slot_dst.reshape(n_tiles, 1, EXPERT_TILE),
                     x1t, w_up, w_down, b_up[:, None, :], b_down[:, None, :], n_assign + 2 * EXPERT_TILE)
    return _combine(gate, x1t, out4t, r2(ln2_g), r2(ln2_b))


def kernel(x, w_in, b_in, attn_sinks, w_attn_br, conv_w, conv_b, conv_ln_g, conv_ln_b, w_conv_br,
           b_conv_br, w_o, ln1_g, ln1_b, w_router, b_router, w_up, b_up, w_down, b_down, ln2_g, ln2_b):
    B, S, D = x.shape
    x2 = x.reshape(B * S, D)
    for l in range(DEPTH):
        x2 = _layer(x2, B, S, w_in[l], b_in[l], attn_sinks[l], w_attn_br[l], conv_w[l], conv_b[l],
                    conv_ln_g[l], conv_ln_b[l], w_conv_br[l], b_conv_br[l], w_o[l], ln1_g[l], ln1_b[l],
                    w_router[l], b_router[l], w_up[l], b_up[l], w_down[l], b_down[l], ln2_g[l], ln2_b[l])
    return x2.reshape(B, S, D)
```

```python
import functools

import jax
import jax.numpy as jnp
from jax import lax
from jax.experimental import pallas as pl
from jax.experimental.pallas import tpu as pltpu

F32 = jnp.float32
BF16 = jnp.bfloat16

D_MODEL = 1024
N_HEADS = 8
N_KV_HEADS = 2
HEAD_DIM = 64
ATTN_WIDTH = N_HEADS * HEAD_DIM
KV_WIDTH = N_KV_HEADS * HEAD_DIM
WINDOW = 128
ATTN_BLOCK = 128
CONV_CH = 512
CONV_WIDTH = 31
Q_END = ATTN_WIDTH
K_END = Q_END + KV_WIDTH
V_END = K_END + KV_WIDTH
CONV_END = V_END + 2 * CONV_CH
IN_WIDTH = CONV_END + 2 * D_MODEL
N_EXPERTS = 32
TOP_K = 4
D_EXPERT = D_MODEL
SWIGLU_ALPHA = 1.702
SWIGLU_LIMIT = 7.0
LN_EPS = 1e-5
DEPTH = 1
DEEPNORM_ALPHA = (2 * DEPTH) ** 0.25

LANES = 128
SUBLANES = 8
NEG_BIG = -1e30
TOKEN_TILE = 512
ATTN_STEP_BLOCKS = 4
ATTN_UNROLL = 2
CONV_HALO = 32
CONV_ROWS = 32
EXPERT_TILE = 256
COMBINE_TILE = 256
CAST_ROWS = 64
DMA_UNROLL = 8
RING = 3
VMEM_LIMIT = 56 * 1024 * 1024


def _layer_norm(z, g, b):
    mu = jnp.mean(z, axis=-1, keepdims=True)
    zc = z - mu
    var = jnp.mean(zc * zc, axis=-1, keepdims=True)
    return zc * lax.rsqrt(var + LN_EPS) * g + b


def _cparams(sem):
    return pltpu.CompilerParams(dimension_semantics=sem, vmem_limit_bytes=VMEM_LIMIT)


def _inproj_kernel(x_ref, w_ref, b_ref, q_ref, k_ref, v_ref, a_ref, g_ref):
    x = x_ref[...].astype(BF16)

    def proj(lo, hi):
        return jnp.dot(x, w_ref[:, lo:hi], preferred_element_type=F32) + b_ref[:, lo:hi]

    qkv = proj(0, V_END)
    q_ref[...] = qkv[:, :Q_END].astype(BF16)
    k_ref[...] = qkv[:, Q_END:K_END].astype(BF16)
    v_ref[...] = qkv[:, K_END:V_END].astype(BF16)
    c = proj(V_END, CONV_END)
    a_ref[...] = c[:, :CONV_CH] * jax.nn.sigmoid(c[:, CONV_CH:])
    g_ref[...] = jax.nn.sigmoid(proj(CONV_END, IN_WIDTH)).astype(BF16)


def _inproj(x2, w_in, b_in):
    T = x2.shape[0]
    row = lambda w: pl.BlockSpec((TOKEN_TILE, w), lambda i: (i, 0))
    full = lambda s: pl.BlockSpec(s, lambda i: (0, 0))
    return pl.pallas_call(
        _inproj_kernel,
        grid=(T // TOKEN_TILE,),
        in_specs=[row(D_MODEL), full((D_MODEL, IN_WIDTH)), full((1, IN_WIDTH))],
        out_specs=[row(ATTN_WIDTH), row(KV_WIDTH), row(KV_WIDTH), row(CONV_CH), row(2 * D_MODEL)],
        out_shape=[jax.ShapeDtypeStruct((T, ATTN_WIDTH), BF16),
                   jax.ShapeDtypeStruct((T, KV_WIDTH), BF16),
                   jax.ShapeDtypeStruct((T, KV_WIDTH), BF16),
                   jax.ShapeDtypeStruct((T, CONV_CH), F32),
                   jax.ShapeDtypeStruct((T, 2 * D_MODEL), BF16)],
        compiler_params=_cparams(("parallel",)),
        name="inproj",
    )(x2, w_in, b_in)


def _attn_kernel(sink_ref, q_ref, kc_ref, kp_ref, vc_ref, vp_ref, o_ref, kcat, vcat):
    i = pl.program_id(1)
    kcat[0:ATTN_BLOCK, :] = kp_ref[...]
    kcat[ATTN_BLOCK:, :] = kc_ref[...]
    vcat[0:ATTN_BLOCK, :] = vp_ref[...]
    vcat[ATTN_BLOCK:, :] = vc_ref[...]
    qi = lax.broadcasted_iota(jnp.int32, (ATTN_BLOCK, 2 * ATTN_BLOCK), 0)
    kj = lax.broadcasted_iota(jnp.int32, (ATTN_BLOCK, 2 * ATTN_BLOCK), 1)
    delta = qi + ATTN_BLOCK - kj
    band = (delta >= 0) & (delta < WINDOW)
    group = N_HEADS // N_KV_HEADS

    def one_block(row0, first):
        mask = band & ((kj >= ATTN_BLOCK) | jnp.logical_not(first))
        q = q_ref[pl.ds(row0, ATTN_BLOCK), :]
        kk = kcat[pl.ds(row0, 2 * ATTN_BLOCK), :]
        vv = vcat[pl.ds(row0, 2 * ATTN_BLOCK), :]
        outs = []
        for h in range(N_HEADS):
            g = h // group
            qh = q[:, h * HEAD_DIM:(h + 1) * HEAD_DIM]
            kh = kk[:, g * HEAD_DIM:(g + 1) * HEAD_DIM]
            vh = vv[:, g * HEAD_DIM:(g + 1) * HEAD_DIM]
            s = lax.dot_general(qh, kh, (((1,), (1,)), ((), ())),
                                preferred_element_type=F32) * (HEAD_DIM ** -0.5)
            s = jnp.where(mask, s, NEG_BIG)
            sink = sink_ref[h]
            m = jnp.maximum(jnp.max(s, axis=-1, keepdims=True), sink)
            p = jnp.exp(s - m)
            denom = jnp.sum(p, axis=-1, keepdims=True) + jnp.exp(sink - m)
            p = p / denom
            outs.append(jnp.dot(p.astype(BF16), vh, preferred_element_type=F32))
        o_ref[pl.ds(row0, ATTN_BLOCK), :] = jnp.concatenate(outs, axis=1).astype(BF16)

    def pair(j, c):
        for u in range(ATTN_UNROLL):
            blk = j * ATTN_UNROLL + u
            one_block(pl.multiple_of(blk * ATTN_BLOCK, ATTN_BLOCK), (i == 0) & (blk == 0))
        return c

    lax.fori_loop(0, ATTN_STEP_BLOCKS // ATTN_UNROLL, pair, 0)


def _attention(q, k, v, sinks, batch, seq):
    T = q.shape[0]
    step = ATTN_STEP_BLOCKS * ATTN_BLOCK
    ns = seq // step
    cur = lambda w: pl.BlockSpec((step, w), lambda b, i: (b * ns + i, 0))
    prev = lambda w: pl.BlockSpec(
        (ATTN_BLOCK, w), lambda b, i: (jnp.maximum((b * ns + i) * ATTN_STEP_BLOCKS - 1, 0), 0))
    return pl.pallas_call(
        _attn_kernel,
        grid=(batch, ns),
        in_specs=[pl.BlockSpec(memory_space=pltpu.SMEM),
                  cur(ATTN_WIDTH), cur(KV_WIDTH), prev(KV_WIDTH), cur(KV_WIDTH), prev(KV_WIDTH)],
        out_specs=cur(ATTN_WIDTH),
        out_shape=jax.ShapeDtypeStruct((T, ATTN_WIDTH), BF16),
        scratch_shapes=[pltpu.VMEM((step + ATTN_BLOCK, KV_WIDTH), BF16),
                        pltpu.VMEM((step + ATTN_BLOCK, KV_WIDTH), BF16)],
        compiler_params=_cparams(("parallel", "parallel")),
        name="swa_attention",
    )(sinks, q, k, k, v, v)


def _conv_kernel(ap_ref, ac_ref, w_ref, cb_ref, g_ref, b_ref, o_ref, ext_ref, sh_ref):
    i = pl.program_id(1)
    ext_ref[0:CONV_HALO, :] = jnp.where(i > 0, ap_ref[...], 0.0)
    ext_ref[CONV_HALO:, :] = ac_ref[...]
    span = sh_ref.shape[1]
    for r in range(1, SUBLANES):
        sh_ref[r - 1] = ext_ref[r:r + span, :]
    base = CONV_HALO - (CONV_WIDTH - 1)

    def rows_chunk(c, carry):
        row0 = pl.multiple_of(c * CONV_ROWS, CONV_ROWS)
        acc = jnp.zeros((CONV_ROWS, CONV_CH), F32) + cb_ref[...]
        for j in range(CONV_WIDTH):
            r = (base + j) % SUBLANES
            rows = pl.ds(row0 + (base + j - r), CONV_ROWS)
            tap = ext_ref[rows, :] if r == 0 else sh_ref[r - 1, rows, :]
            acc = acc + tap * w_ref[j:j + 1, :]
        ext_ref[pl.ds(row0, CONV_ROWS), :] = acc
        return carry

    lax.fori_loop(0, TOKEN_TILE // CONV_ROWS, rows_chunk, 0)
    y = _layer_norm(ext_ref[0:TOKEN_TILE, :], g_ref[...], b_ref[...])
    o_ref[...] = (y * jax.nn.sigmoid(y)).astype(BF16)


def _conv_branch(a, conv_w, conv_b, ln_g, ln_b, batch, seq):
    T = a.shape[0]
    nt = seq // TOKEN_TILE
    halo_per_tile = TOKEN_TILE // CONV_HALO
    cur = pl.BlockSpec((TOKEN_TILE, CONV_CH), lambda b, i: (b * nt + i, 0))
    prev = pl.BlockSpec((CONV_HALO, CONV_CH),
                        lambda b, i: (jnp.maximum((b * nt + i) * halo_per_tile - 1, 0), 0))
    full = lambda s: pl.BlockSpec(s, lambda b, i: (0, 0))
    return pl.pallas_call(
        _conv_kernel,
        grid=(batch, nt),
        in_specs=[prev, cur, full((CONV_WIDTH, CONV_CH)), full((1, CONV_CH)),
                  full((1, CONV_CH)), full((1, CONV_CH))],
        out_specs=cur,
        out_shape=jax.ShapeDtypeStruct((T, CONV_CH), BF16),
        scratch_shapes=[pltpu.VMEM((CONV_HALO + TOKEN_TILE, CONV_CH), F32),
                        pltpu.VMEM((SUBLANES - 1, CONV_HALO + TOKEN_TILE - SUBLANES, CONV_CH), F32)],
        compiler_params=_cparams(("parallel", "parallel")),
        name="conformer_conv",
    )(a, a, conv_w, conv_b, ln_g, ln_b)


def _store_token_tiles(ref, x, lead=()):
    rows = x.shape[0]
    for q in range(SUBLANES):
        ref[lead + (pl.ds(q, rows, stride=SUBLANES), slice(None))] = x[:, q * LANES:(q + 1) * LANES]


def _load_token_tiles(ref, rows, lead=()):
    return jnp.concatenate(
        [ref[lead + (pl.ds(q, rows, stride=SUBLANES), slice(None))] for q in range(SUBLANES)], axis=1)


def _split_bf16(x):
    hi = x.astype(BF16)
    lo = (x - hi.astype(F32)).astype(BF16)
    return hi, lo


def _merge_kernel(o_ref, yc_ref, g_ref, x_ref, wa_ref, wc_ref, bc_ref, wo_ref, lng_ref, lnb_ref,
                  wrh_ref, wrl_ref, br_ref, x1_ref, lg_ref):
    ya = jnp.dot(o_ref[...], wa_ref[...], preferred_element_type=F32)
    yc = jnp.dot(yc_ref[...], wc_ref[...], preferred_element_type=F32) + bc_ref[...]
    merged = g_ref[:, :D_MODEL] * ya + g_ref[:, D_MODEL:] * yc
    z = DEEPNORM_ALPHA * x_ref[...] + jnp.dot(merged.astype(BF16), wo_ref[...],
                                              preferred_element_type=F32)
    x1 = _layer_norm(z, lng_ref[...], lnb_ref[...])
    _store_token_tiles(x1_ref, x1)
    hi, lo = _split_bf16(x1)
    wh = wrh_ref[...]
    lg = (jnp.dot(hi, wh, preferred_element_type=F32)
          + (jnp.dot(hi, wrl_ref[...], preferred_element_type=F32)
             + jnp.dot(lo, wh, preferred_element_type=F32)))
    lg_ref[...] = lg + br_ref[...]


def _merge(o, yc, g, x2, wa, wc, bc, wo, lng, lnb, wrh, wrl, br):
    T = x2.shape[0]
    row = lambda w: pl.BlockSpec((TOKEN_TILE, w), lambda i: (i, 0))
    full = lambda s: pl.BlockSpec(s, lambda i: (0, 0))
    return pl.pallas_call(
        _merge_kernel,
        grid=(T // TOKEN_TILE,),
        in_specs=[row(ATTN_WIDTH), row(CONV_CH), row(2 * D_MODEL), row(D_MODEL),
                  full((ATTN_WIDTH, D_MODEL)), full((CONV_CH, D_MODEL)), full((1, D_MODEL)),
                  full((D_MODEL, D_MODEL)), full((1, D_MODEL)), full((1, D_MODEL)),
                  full((D_MODEL, LANES)), full((D_MODEL, LANES)), full((1, LANES))],
        out_specs=[pl.BlockSpec((TOKEN_TILE * SUBLANES, LANES), lambda i: (i, 0)), row(LANES)],
        out_shape=[jax.ShapeDtypeStruct((T * SUBLANES, LANES), F32), jax.ShapeDtypeStruct((T, LANES), F32)],
        compiler_params=_cparams(("parallel",)),
        name="merge_ln1_router",
    )(o, yc, g, x2, wa, wc, bc, wo, lng, lnb, wrh, wrl, br)


def _route_kernel(lg_ref, idx_ref, gate_ref, rank_ref, cnt_ref, carry_ref):
    @pl.when(pl.program_id(0) == 0)
    def _():
        carry_ref[...] = jnp.zeros_like(carry_ref)

    work = lg_ref[...]
    shape = work.shape
    lane_i = lax.broadcasted_iota(jnp.int32, shape, 1)
    lane = lane_i.astype(F32)
    vals, hots, firsts = [], [], []
    for _ in range(TOP_K):
        m = jnp.max(work, axis=-1, keepdims=True)
        first = jnp.min(jnp.where(work == m, lane, float(LANES)), axis=-1, keepdims=True)
        hot = lane == first
        vals.append(m)
        hots.append(hot)
        firsts.append(first)
        work = jnp.where(hot, -jnp.inf, work)
    exps = [jnp.exp(v - vals[0]) for v in vals]
    denom = exps[0] + exps[1] + exps[2] + exps[3]
    sel = hots[0] | hots[1] | hots[2] | hots[3]
    sel_f = jnp.where(sel, 1.0, 0.0)
    row = lax.broadcasted_iota(jnp.int32, (shape[0], shape[0]), 0)
    col = lax.broadcasted_iota(jnp.int32, (shape[0], shape[0]), 1)
    earlier = jnp.where(row > col, 1.0, 0.0).astype(BF16)
    prefix = jnp.dot(earlier, sel_f.astype(BF16), preferred_element_type=F32) + carry_ref[...]
    idx = jnp.zeros(shape, F32)
    gate = jnp.zeros(shape, F32)
    rank = jnp.zeros(shape, F32)
    for k in range(TOP_K):
        here = lane_i == k
        rank_k = jnp.sum(jnp.where(hots[k], prefix, 0.0), axis=-1, keepdims=True)
        idx = jnp.where(here, firsts[k], idx)
        gate = jnp.where(here, exps[k] / denom, gate)
        rank = jnp.where(here, rank_k, rank)
    idx_ref[...] = idx.astype(jnp.int32)
    gate_ref[...] = gate
    rank_ref[...] = rank.astype(jnp.int32)
    carry_ref[...] = carry_ref[...] + jnp.sum(sel_f, axis=0, keepdims=True)
    cnt_ref[...] = carry_ref[...].astype(jnp.int32)


def _route(logits):
    T = logits.shape[0]
    row = pl.BlockSpec((TOKEN_TILE, LANES), lambda i: (i, 0))
    return pl.pallas_call(
        _route_kernel,
        grid=(T // TOKEN_TILE,),
        in_specs=[row],
        out_specs=[row, row, row, pl.BlockSpec((1, LANES), lambda i: (0, 0))],
        out_shape=[jax.ShapeDtypeStruct((T, LANES), jnp.int32),
                   jax.ShapeDtypeStruct((T, LANES), F32),
                   jax.ShapeDtypeStruct((T, LANES), jnp.int32),
                   jax.ShapeDtypeStruct((1, LANES), jnp.int32)],
        scratch_shapes=[pltpu.VMEM((1, LANES), F32)],
        compiler_params=_cparams(("arbitrary",)),
        name="route_topk_rank",
    )(logits)


def _expert_kernel(te_ref, first_ref, nxt_ref, nu_ref,
                   tok0_ref, tok1_ref, tok2_ref, dstp_ref, dstc_ref, x1_hbm, wup_hbm, wdn_hbm, bup_ref, bdn_ref,
                   out_hbm,
                   wup_f32, wdn_f32, wdn_perm, wup_bf, wdn_bf, xbuf, ybuf, wsem, gsem, ssem):
    s = pl.program_id(0)
    n_used = nu_ref[0]
    slot = lax.rem(s, RING)
    slot_p1 = lax.rem(s + 1, RING)
    slot_p2 = lax.rem(s + 2, RING)
    tile_rows = EXPERT_TILE * SUBLANES
    n_chunks = D_EXPERT // LANES

    def weight_copies(e):
        return (pltpu.make_async_copy(wup_hbm.at[e], wup_f32, wsem.at[0]),
                pltpu.make_async_copy(wdn_hbm.at[e], wdn_f32, wsem.at[1]))

    def token_copy(src, dst, sem, src_tok, dst_tok):
        rows = lambda t: pl.ds(pl.multiple_of(t * SUBLANES, SUBLANES), SUBLANES)
        return pltpu.make_async_copy(src.at[rows(src_tok), :], dst.at[rows(dst_tok), :], sem)

    def wait_tokens(sem):
        def body(g, c):
            for _ in range(DMA_UNROLL):
                token_copy(x1_hbm, xbuf.at[0], sem, 0, 0).wait()
            return c
        lax.fori_loop(0, EXPERT_TILE // DMA_UNROLL, body, 0)

    @pl.when(s == 0)
    def _():
        for cp in weight_copies(te_ref[0]):
            cp.start()

        def first_gathers(g, c):
            for j in range(DMA_UNROLL):
                r = g * DMA_UNROLL + j
                token_copy(x1_hbm, xbuf.at[0], gsem.at[0], tok0_ref[0, 0, r], r).start(priority=0)
                token_copy(x1_hbm, xbuf.at[1], gsem.at[1], tok1_ref[0, 0, r], r).start(priority=1)
            return c

        lax.fori_loop(0, EXPERT_TILE // DMA_UNROLL, first_gathers, 0)
        ybuf[...] = jnp.zeros_like(ybuf)
        spare0 = out_hbm.shape[0] - 2 * tile_rows
        fills = [pltpu.make_async_copy(ybuf.at[b], out_hbm.at[pl.ds(spare0 + b * tile_rows, tile_rows), :],
                                       ssem.at[b]) for b in range(2)]
        for cp in fills:
            cp.start()
        for cp in fills:
            cp.wait()

    @pl.when(s < n_used)
    def _():
        @pl.when(first_ref[s] == 1)
        def _():
            for cp in weight_copies(te_ref[s]):
                cp.wait()

            def cast_up(i, c):
                r = pl.multiple_of(i * CAST_ROWS, CAST_ROWS)
                wup_bf[pl.ds(r, CAST_ROWS), :] = wup_f32[pl.ds(r, CAST_ROWS), :].astype(BF16)
                return c

            lax.fori_loop(0, D_MODEL // CAST_ROWS, cast_up, 0)

            half = LANES // 2

            def permute_down(c, carry):
                r = pl.multiple_of(c * LANES, LANES)
                for j in range(n_chunks):
                    cols = slice(j * LANES, (j + 1) * LANES)
                    wdn_perm[j, pl.ds(r, half, stride=2), :] = wdn_f32[pl.ds(r, half), cols]
                    wdn_perm[j, pl.ds(r + 1, half, stride=2), :] = wdn_f32[pl.ds(r + half, half), cols]
                return carry

            lax.fori_loop(0, n_chunks, permute_down, 0)

            def cast_down(i, c):
                r = pl.multiple_of(i * CAST_ROWS, CAST_ROWS)
                for j in range(n_chunks):
                    wdn_bf[j, pl.ds(r, CAST_ROWS), :] = wdn_perm[j, pl.ds(r, CAST_ROWS), :].astype(BF16)
                return c

            lax.fori_loop(0, D_EXPERT // CAST_ROWS, cast_down, 0)

            @pl.when(nxt_ref[s] >= 0)
            def _():
                for cp in weight_copies(nxt_ref[s]):
                    cp.start()

        wait_tokens(gsem.at[slot])

        @pl.when(s >= 2)
        def _():
            wait_tokens(ssem.at[slot])

        x = _load_token_tiles(xbuf, EXPERT_TILE, (slot,)).astype(BF16)
        for r in range(EXPERT_TILE):
            token_copy(x1_hbm, xbuf.at[slot_p2], gsem.at[slot_p2], tok2_ref[0, 0, r], r).start(priority=0)
            token_copy(ybuf.at[slot_p2], out_hbm, ssem.at[slot_p2], r, dstp_ref[0, 0, r]).start(priority=1)
        up = jnp.dot(x, wup_bf[...], preferred_element_type=F32) + bup_ref[0]
        even = (lax.broadcasted_iota(jnp.int32, (EXPERT_TILE, LANES), 1) & 1) == 0
        acts = []
        for c in range(D_EXPERT // LANES):
            a = up[:, 2 * c * LANES:(2 * c + 1) * LANES]
            b = up[:, (2 * c + 1) * LANES:(2 * c + 2) * LANES]
            glu = jnp.where(even, a, pltpu.roll(b, 1, axis=1))
            lin = jnp.where(even, pltpu.roll(a, LANES - 1, axis=1), b)
            glu = jnp.minimum(glu, SWIGLU_LIMIT)
            lin = jnp.clip(lin, -SWIGLU_LIMIT, SWIGLU_LIMIT)
            acts.append((glu * jax.nn.sigmoid(SWIGLU_ALPHA * glu) * (lin + 1.0)).astype(BF16))
        act = jnp.concatenate(acts, axis=1)
        wdn = jnp.concatenate([wdn_bf[j] for j in range(n_chunks)], axis=1)
        y = jnp.dot(act, wdn, preferred_element_type=F32) + bdn_ref[0]
        _store_token_tiles(ybuf, y, (slot,))

        @pl.when(s == n_used - 1)
        def _():
            def last_scatter(g, c):
                for j in range(DMA_UNROLL):
                    r = g * DMA_UNROLL + j
                    token_copy(ybuf.at[slot], out_hbm, ssem.at[slot], r, dstc_ref[0, 0, r]).start(priority=j % 2)
                return c

            lax.fori_loop(0, EXPERT_TILE // DMA_UNROLL, last_scatter, 0)

            @pl.when(s >= 1)
            def _():
                wait_tokens(ssem.at[slot_p1])
            wait_tokens(ssem.at[slot_p2])
            wait_tokens(ssem.at[slot])
            wait_tokens(gsem.at[slot_p1])
            wait_tokens(gsem.at[slot_p2])


def _experts(tile_expert, tile_first, tile_next, n_used, slot_tok3, slot_dst3, x1t, w_up, w_down,
             bup, bdn, n_out_tokens):
    n_tiles = slot_tok3.shape[0]
    spare = (n_out_tokens - EXPERT_TILE + jnp.arange(EXPERT_TILE, dtype=jnp.int32)).reshape(1, 1, EXPERT_TILE)
    slot_dst_prev3 = jnp.concatenate([spare, slot_dst3[:-1]], axis=0)
    tile_rows = EXPERT_TILE * SUBLANES
    smem_tile = lambda f: pl.BlockSpec((1, 1, EXPERT_TILE), f, memory_space=pltpu.SMEM)
    by_expert = lambda w: pl.BlockSpec((1, 1, w), lambda s, te, *_: (te[s], 0, 0))
    grid_spec = pltpu.PrefetchScalarGridSpec(
        num_scalar_prefetch=4,
        grid=(n_tiles,),
        in_specs=[smem_tile(lambda s, *_: (s, 0, 0)),
                  smem_tile(lambda s, *_: (jnp.minimum(s + 1, n_tiles - 1), 0, 0)),
                  smem_tile(lambda s, *_: (jnp.minimum(s + 2, n_tiles - 1), 0, 0)),
                  smem_tile(lambda s, *_: (s, 0, 0)),
                  smem_tile(lambda s, *_: (s, 0, 0)),
                  pl.BlockSpec(memory_space=pl.ANY),
                  pl.BlockSpec(memory_space=pl.ANY),
                  pl.BlockSpec(memory_space=pl.ANY),
                  by_expert(2 * D_EXPERT), by_expert(D_MODEL)],
        out_specs=pl.BlockSpec(memory_space=pl.ANY),
        scratch_shapes=[pltpu.VMEM((D_MODEL, 2 * D_EXPERT), F32),
                        pltpu.VMEM((D_EXPERT, D_MODEL), F32),
                        pltpu.VMEM((D_MODEL // LANES, D_EXPERT, LANES), F32),
                        pltpu.VMEM((D_MODEL, 2 * D_EXPERT), BF16),
                        pltpu.VMEM((D_MODEL // LANES, D_EXPERT, LANES), BF16),
                        pltpu.VMEM((RING, tile_rows, LANES), F32),
                        pltpu.VMEM((RING, tile_rows, LANES), F32),
                        pltpu.SemaphoreType.DMA((2,)),
                        pltpu.SemaphoreType.DMA((RING,)),
                        pltpu.SemaphoreType.DMA((RING,))],
    )
    return pl.pallas_call(
        _expert_kernel,
        grid_spec=grid_spec,
        out_shape=jax.ShapeDtypeStruct((n_out_tokens * SUBLANES, LANES), F32),
        compiler_params=_cparams(("arbitrary",)),
        name="expert_ffn",
    )(tile_expert, tile_first, tile_next, n_used, slot_tok3, slot_tok3, slot_tok3, slot_dst_prev3, slot_dst3,
      x1t, w_up, w_down, bup, bdn)


def _combine_kernel(gate_ref, x1_ref, y0_ref, y1_ref, y2_ref, y3_ref, lng_ref, lnb_ref, out_ref):
    gate = gate_ref[...]
    zs = []
    for q in range(SUBLANES):
        rows = pl.ds(q, COMBINE_TILE, stride=SUBLANES)
        y = gate[:, 0:1] * y0_ref[rows, :]
        for k, y_ref in ((1, y1_ref), (2, y2_ref), (3, y3_ref)):
            y = y + gate[:, k:k + 1] * y_ref[rows, :]
        zs.append(DEEPNORM_ALPHA * x1_ref[rows, :] + y)
    out_ref[...] = _layer_norm(jnp.concatenate(zs, axis=1), lng_ref[...], lnb_ref[...])


def _combine(gate, x1t, out4t, lng, lnb):
    T = gate.shape[0]
    nt = T // COMBINE_TILE
    row = lambda w: pl.BlockSpec((COMBINE_TILE, w), lambda i: (i, 0))
    tiles = lambda k: pl.BlockSpec((COMBINE_TILE * SUBLANES, LANES), lambda i: (k * nt + i, 0))
    full = lambda s: pl.BlockSpec(s, lambda i: (0, 0))
    return pl.pallas_call(
        _combine_kernel,
        grid=(nt,),
        in_specs=[row(LANES), tiles(0), tiles(0), tiles(1), tiles(2), tiles(3),
                  full((1, D_MODEL)), full((1, D_MODEL))],
        out_specs=row(D_MODEL),
        out_shape=jax.ShapeDtypeStruct((T, D_MODEL), F32),
        compiler_params=_cparams(("parallel",)),
        name="combine_ln2",
    )(gate, x1t, out4t, out4t, out4t, out4t, lng, lnb)


def _windows_kernel(c0_ref, comp_ref, out_ref):
    lane = lax.broadcasted_iota(jnp.int32, (SUBLANES, LANES), 1)
    span = 2 * SUBLANES

    def body(s, c):
        c0 = c0_ref[s]
        row = lax.shift_right_logical(c0, LANES.bit_length() - 1)
        off = c0 & (LANES - 1)
        row8 = pl.multiple_of(lax.shift_right_logical(row, SUBLANES.bit_length() - 1) * SUBLANES, SUBLANES)
        blk = comp_ref[pl.ds(row8, span), :]
        x = pltpu.roll(blk, (span - (row - row8)) & (span - 1), axis=0)[:SUBLANES]
        r = pltpu.roll(x, (LANES - off) & (LANES - 1), axis=1)
        next_row = pltpu.roll(r, SUBLANES - 1, axis=0)
        out_ref[pl.ds(pl.multiple_of(s * SUBLANES, SUBLANES), SUBLANES), :] = jnp.where(lane < LANES - off, r, next_row)
        return c

    lax.fori_loop(0, c0_ref.shape[0], body, 0)


def _slot_windows(tile_c0, compact_a):
    assert EXPERT_TILE == 2 * LANES and compact_a.shape[0] % LANES == 0
    n_tiles = tile_c0.shape[0]
    comp = jnp.concatenate([compact_a, jnp.zeros((2 * SUBLANES * LANES,), compact_a.dtype)]).reshape(-1, LANES)
    out = pl.pallas_call(
        _windows_kernel,
        out_shape=jax.ShapeDtypeStruct((n_tiles * SUBLANES, LANES), compact_a.dtype),
        in_specs=[pl.BlockSpec(memory_space=pltpu.SMEM), pl.BlockSpec(comp.shape, lambda: (0, 0))],
        out_specs=pl.BlockSpec((n_tiles * SUBLANES, LANES), lambda: (0, 0)),
        name="slot_windows",
    )(tile_c0, comp)
    return out.reshape(n_tiles, SUBLANES, LANES)[:, :EXPERT_TILE // LANES, :].reshape(n_tiles, EXPERT_TILE)


def _lookup(table, idx):
    hit = idx[..., None] == jnp.arange(table.shape[0], dtype=idx.dtype)
    return jnp.sum(jnp.where(hit, table, 0), axis=-1)


def _layer(x2, batch, seq, w_in, b_in, sinks, w_attn_br, conv_w, conv_b, conv_ln_g, conv_ln_b,
           w_conv_br, b_conv_br, w_o, ln1_g, ln1_b, w_router, b_router, w_up, b_up, w_down, b_down,
           ln2_g, ln2_b):
    T = x2.shape[0]
    r2 = lambda v: v.reshape(1, -1)
    q, k, v, a, g = _inproj(x2, w_in.astype(BF16), r2(b_in))
    o = _attention(q, k, v, sinks, batch, seq)
    yc = _conv_branch(a, conv_w, r2(conv_b), r2(conv_ln_g), r2(conv_ln_b), batch, seq)

    wr = jnp.pad(w_router, ((0, 0), (0, LANES - N_EXPERTS)))
    wr_hi, wr_lo = _split_bf16(wr)
    br = jnp.pad(b_router, (0, LANES - N_EXPERTS), constant_values=NEG_BIG)
    x1t, logits = _merge(o, yc, g, x2, w_attn_br.astype(BF16), w_conv_br.astype(BF16), r2(b_conv_br),
                        w_o.astype(BF16), r2(ln1_g), r2(ln1_b), wr_hi, wr_lo, r2(br))

    i32 = jnp.int32
    idx, gate, rank, cnt = _route(logits)
    n_assign = T * TOP_K
    n_pad = N_EXPERTS * EXPERT_TILE
    n_slots = n_assign + n_pad
    n_tiles = n_slots // EXPERT_TILE
    counts = cnt[0, :N_EXPERTS]
    padded = (counts + EXPERT_TILE - 1) // EXPERT_TILE * EXPERT_TILE
    pad_end = jnp.cumsum(padded)
    pad_start = pad_end - padded
    pos = _lookup(pad_start, idx[:, :TOP_K]) + rank[:, :TOP_K]
    _, compact_a = lax.sort_key_val(pos.reshape(-1).astype(i32), jnp.arange(n_assign, dtype=i32))

    tile_start = jnp.arange(n_tiles, dtype=i32) * EXPERT_TILE
    tile_expert = jnp.minimum(jnp.sum(pad_end[None, :] <= tile_start[:, None], axis=1), N_EXPERTS - 1).astype(i32)
    n_used = (pad_end[-1] // EXPERT_TILE).astype(i32)
    tile_first = (tile_start == _lookup(pad_start, tile_expert)).astype(i32)
    group_end = _lookup(pad_end, tile_expert) // EXPERT_TILE
    tile_next = jnp.where(group_end < n_used,
                          jnp.take(tile_expert, jnp.minimum(group_end, n_tiles - 1)), -1).astype(i32)

    tile_row0 = tile_start - _lookup(pad_start, tile_expert)
    tile_c0 = jnp.clip(_lookup(jnp.cumsum(counts) - counts, tile_expert) + tile_row0, 0, n_assign)
    tile_valid = jnp.clip(_lookup(counts, tile_expert) - tile_row0, 0, EXPERT_TILE)
    slot_a = _slot_windows(tile_c0, compact_a)
    in_tile = jnp.arange(EXPERT_TILE, dtype=i32)[None, :]
    real = in_tile < tile_valid[:, None]
    slot_tok = jnp.where(real, slot_a // TOP_K, 0)
    spare_row = n_assign + (tile_start[:, None] + in_tile) % (2 * EXPERT_TILE)
    slot_dst = jnp.where(real, (slot_a % TOP_K) * T + slot_a // TOP_K, spare_row)
    out4t = _experts(tile_expert, tile_first, tile_next, n_used.reshape(1),
                     slot_tok.reshape(n_tiles, 1, EXPERT_TILE), slot_dst.reshape(n_tiles, 1, EXPERT_TILE),
                     x1t, w_up, w_down, b_up[:, None, :], b_down[:, None, :], n_assign + 2 * EXPERT_TILE)
    return _combine(gate, x1t, out4t, r2(ln2_g), r2(ln2_b))


def kernel(x, w_in, b_in, attn_sinks, w_attn_br, conv_w, conv_b, conv_ln_g, conv_ln_b, w_conv_br,
           b_conv_br, w_o, ln1_g, ln1_b, w_router, b_router, w_up, b_up, w_down, b_down, ln2_g, ln2_b):
    B, S, D = x.shape
    x2 = x.reshape(B * S, D)
    for l in range(DEPTH):
        x2 = _layer(x2, B, S, w_in[l], b_in[l], attn_sinks[l], w_attn_br[l], conv_w[l], conv_b[l],
                    conv_ln_g[l], conv_ln_b[l], w_conv_br[l], b_conv_br[l], w_o[l], ln1_g[l], ln1_b[l],
                    w_router[l], b_router[l], w_up[l], b_up[l], w_down[l], b_down[l], ln2_g[l], ln2_b[l])
    return x2.reshape(B, S, D)
```
